```python
import math
import jax, jax.numpy as jnp
from jax import lax
import numpy as np

D_MODEL = 2048
BATCH = 1
SEQ = 8192
DEPTH = 1
DEC_BATCH = 32
DEC_SEQ = 1
PAST_LEN = 8192
PAGE_SIZE = 128

HEAD_DIM = 128
H_DIFF = 8
DH_HALF = HEAD_DIM // 2
H_MLSTM = 4
H_MEM = 4
N_MEM = 256
D_DIFF = H_DIFF * HEAD_DIM
D_MLSTM = H_MLSTM * HEAD_DIM
D_MEMH = H_MEM * HEAD_DIM
D_MIX = D_DIFF + D_MLSTM + D_MEMH
D_IN = 3 * D_DIFF + 4 * D_MLSTM + 2 * H_MLSTM + D_MEMH
D_FF = 5632
ROPE_THETA = 10000.0
EPS = 1e-6
Q_BLOCK = 128
MLSTM_CHUNK = 128
NEG = -1e30

kernel_name = 'hybrid_diffattn_mlstm_memory_step'


def rms_norm(x, g):
    xf = x.astype(jnp.float32)
    y = xf * lax.rsqrt(jnp.mean(xf * xf, axis=-1, keepdims=True) + EPS)
    return (y * g.astype(jnp.float32)).astype(x.dtype)


def swiglu_half(x, g, w_gate, w_up, w_down):
    h = rms_norm(x, g)
    return x + 0.5 * ((jax.nn.silu(h @ w_gate) * (h @ w_up)) @ w_down)


def rope_tables(pos, dim):
    inv = 1.0 / (ROPE_THETA ** (jnp.arange(0, dim, 2, dtype=jnp.float32) / dim))
    ang = pos.astype(jnp.float32)[:, None] * inv[None, :]
    return jnp.cos(ang), jnp.sin(ang)


def apply_rope(x, cos, sin):
    half = x.shape[-1] // 2
    c = cos[None, :, None, None, :]
    s = sin[None, :, None, None, :]
    x1 = x[..., :half].astype(jnp.float32)
    x2 = x[..., half:].astype(jnp.float32)
    return jnp.concatenate([x1 * c - x2 * s, x2 * c + x1 * s], axis=-1).astype(x.dtype)


def mix_project(h, pos, w_in, b_igate, b_fgate, g_q_diff, g_k_diff, g_q_mem):
    B, T, _ = h.shape
    z = h @ w_in
    sizes = (D_DIFF, D_DIFF, D_DIFF, D_MLSTM, D_MLSTM, D_MLSTM, D_MLSTM, H_MLSTM, H_MLSTM, D_MEMH)
    idx = np.cumsum(sizes)[:-1].tolist()
    dq, dk, dv, lq, lk, lv, lo, li, lf, mq = jnp.split(z, idx, axis=-1)
    cos, sin = rope_tables(pos, DH_HALF)
    dq = apply_rope(rms_norm(dq.reshape(B, T, H_DIFF, 2, DH_HALF), g_q_diff), cos, sin)
    dk = apply_rope(rms_norm(dk.reshape(B, T, H_DIFF, 2, DH_HALF), g_k_diff), cos, sin)
    dv = dv.reshape(B, T, H_DIFF, HEAD_DIM)
    f32 = jnp.float32
    lq = lq.reshape(B, T, H_MLSTM, HEAD_DIM).astype(f32)
    lk = lk.reshape(B, T, H_MLSTM, HEAD_DIM).astype(f32) * (HEAD_DIM ** -0.5)
    lv = lv.reshape(B, T, H_MLSTM, HEAD_DIM).astype(f32)
    ig = (li + b_igate).astype(f32)
    lfg = jax.nn.log_sigmoid((lf + b_fgate).astype(f32))
    mq = rms_norm(mq.reshape(B, T, H_MEM, HEAD_DIM), g_q_mem)
    return dq, dk, dv, lq, lk, lv, lo, ig, lfg, mq


def diff_attn_prompt(q, k, v, lam):
    B, S = q.shape[0], q.shape[1]
    nb = S // Q_BLOCK
    scale = DH_HALF ** -0.5
    kpos = jnp.arange(S)
    qb = jnp.moveaxis(q.reshape(B, nb, Q_BLOCK, H_DIFF, 2, DH_HALF), 1, 0)

    def block(args):
        qi, bi = args
        s = jnp.einsum('bqhcd,bkhcd->bhcqk', qi, k, preferred_element_type=jnp.float32) * scale
        qpos = bi * Q_BLOCK + jnp.arange(Q_BLOCK)
        s = jnp.where(kpos[None, :] <= qpos[:, None], s, NEG)
        p = jax.nn.softmax(s, axis=-1)
        a = p[:, :, 0] - lam * p[:, :, 1]
        return jnp.einsum('bhqk,bkhd->bqhd', a.astype(v.dtype), v)

    out = lax.map(block, (qb, jnp.arange(nb)))
    return jnp.moveaxis(out, 0, 1).reshape(B, S, H_DIFF, HEAD_DIM)


def diff_attn_sample(q, k_new, v_new, k_past, v_past, lam):
    T = q.shape[1]
    P = k_past.shape[1]
    scale = DH_HALF ** -0.5
    s_past = jnp.einsum('bqhcd,bkhcd->bhcqk', q, k_past, preferred_element_type=jnp.float32) * scale
    s_new = jnp.einsum('bqhcd,bkhcd->bhcqk', q, k_new, preferred_element_type=jnp.float32) * scale
    s_new = jnp.where(jnp.tril(jnp.ones((T, T), dtype=bool)), s_new, NEG)
    p = jax.nn.softmax(jnp.concatenate([s_past, s_new], axis=-1), axis=-1)
    a = (p[:, :, 0] - lam * p[:, :, 1]).astype(v_past.dtype)
    return (jnp.einsum('bhqk,bkhd->bqhd', a[..., :P], v_past)
            + jnp.einsum('bhqk,bkhd->bqhd', a[..., P:], v_new))


def mlstm_chunk(carry, xs):
    C0, n0, m0 = carry
    q, k, v, ig, lf = xs
    L = q.shape[1]
    bt = jnp.cumsum(lf, axis=1).transpose(0, 2, 1)
    igt = ig.transpose(0, 2, 1)
    dm = bt[..., :, None] - bt[..., None, :] + igt[..., None, :]
    dm = jnp.where(jnp.tril(jnp.ones((L, L), dtype=bool)), dm, NEG)
    inter = bt + m0[..., None]
    m = jnp.maximum(inter, jnp.max(dm, axis=-1))
    w_intra = jnp.exp(dm - m[..., None])
    w_inter = jnp.exp(inter - m)
    A = w_intra * jnp.einsum('bthd,bshd->bhts', q, k)
    num = (jnp.einsum('bhts,bshd->bthd', A, v)
           + w_inter.transpose(0, 2, 1)[..., None] * jnp.einsum('bhvk,bthk->bthv', C0, q))
    den = jnp.sum(A, axis=-1) + w_inter * jnp.einsum('bhk,bthk->bht', n0, q)
    denom = jnp.maximum(jnp.abs(den), jnp.exp(-m))
    h = num / denom.transpose(0, 2, 1)[..., None]
    bl = bt[..., -1]
    ml = m[..., -1]
    w_end = jnp.exp(bl[..., None] - bt + igt - ml[..., None])
    decay = jnp.exp(bl + m0 - ml)
    C1 = decay[..., None, None] * C0 + jnp.einsum('bhs,bshv,bshk->bhvk', w_end, v, k)
    n1 = decay[..., None] * n0 + jnp.einsum('bhs,bshk->bhk', w_end, k)
    return (C1, n1, ml), h


def mlstm_prompt(q, k, v, ig, lf):
    B, S, H, D = q.shape
    nc = S // MLSTM_CHUNK

    def to_chunks(a):
        return jnp.moveaxis(a.reshape((B, nc, MLSTM_CHUNK) + a.shape[2:]), 1, 0)

    f32 = jnp.float32
    init = (jnp.zeros((B, H, D, D), f32), jnp.zeros((B, H, D), f32), jnp.zeros((B, H), f32))
    state, h = lax.scan(mlstm_chunk, init, (to_chunks(q), to_chunks(k), to_chunks(v), to_chunks(ig), to_chunks(lf)))
    return jnp.moveaxis(h, 0, 1).reshape(B, S, H, D), state


def memory_kv(mem, g_mem_in, w_mem_k, w_mem_v, g_k_mem):
    B, M, _ = mem.shape
    mn = rms_norm(mem, g_mem_in)
    mk = rms_norm((mn @ w_mem_k).reshape(B, M, H_MEM, HEAD_DIM), g_k_mem)
    mv = (mn @ w_mem_v).reshape(B, M, H_MEM, HEAD_DIM)
    return mk, mv


def memory_attend(q, mk, mv):
    s = jnp.einsum('bthd,bmhd->bhtm', q, mk, preferred_element_type=jnp.float32) * (HEAD_DIM ** -0.5)
    p = jax.nn.softmax(s, axis=-1)
    return jnp.einsum('bhtm,bmhd->bthd', p.astype(mv.dtype), mv)


def mix_output(x, diff_o, h_m, lo, mem_o, g_subln, lambda_init, g_mlstm_out, w_out):
    B, T, _ = x.shape
    d = (rms_norm(diff_o, g_subln) * (1.0 - lambda_init)).reshape(B, T, D_DIFF)
    m = rms_norm(h_m, g_mlstm_out).astype(x.dtype).reshape(B, T, D_MLSTM) * jax.nn.sigmoid(lo)
    c = mem_o.reshape(B, T, D_MEMH).astype(x.dtype)
    return x + jnp.concatenate([d, m, c], axis=-1) @ w_out


def _normal(key, shape, scale):
    return jax.random.normal(key, shape, jnp.float32) * scale


def _gain(key, shape):
    return 1.0 + 0.02 * jax.random.normal(key, shape, jnp.float32)


def setup_inputs(seed: int = 0) -> dict:
    key = jax.random.key(seed)
    ks = jax.random.split(key, 40)
    n_pages = PAST_LEN // PAGE_SIZE
    n_used = DEC_BATCH * n_pages
    n_phys = n_used + max(1, n_used // 4)
    page_table = jax.random.permutation(ks[0], n_phys)[:n_used].reshape(DEC_BATCH, n_pages).astype(jnp.int32)
    d = D_MODEL
    return {
        'x_prompt': _normal(ks[1], (BATCH, SEQ, d), 1.0),
        'x_sample': _normal(ks[2], (DEC_BATCH, DEC_SEQ, d), 1.0),
        'cache_diff_k': _normal(ks[3], (DEPTH, n_phys, PAGE_SIZE, H_DIFF, 2, DH_HALF), 1.0),
        'cache_diff_v': _normal(ks[4], (DEPTH, n_phys, PAGE_SIZE, H_DIFF, HEAD_DIM), 1.0),
        'cache_mem_k': _normal(ks[5], (DEPTH, DEC_BATCH, N_MEM, H_MEM, HEAD_DIM), 1.0),
        'cache_mem_v': _normal(ks[6], (DEPTH, DEC_BATCH, N_MEM, H_MEM, HEAD_DIM), 1.0),
        'state_mlstm_C': _normal(ks[7], (DEPTH, DEC_BATCH, H_MLSTM, HEAD_DIM, HEAD_DIM), 0.1),
        'state_mlstm_n': _normal(ks[8], (DEPTH, DEC_BATCH, H_MLSTM, HEAD_DIM), 0.1),
        'state_mlstm_m': _normal(ks[9], (DEPTH, DEC_BATCH, H_MLSTM), 1.0),
        'page_table': page_table,
        'mem_prompt': _normal(ks[10], (BATCH, N_MEM, d), 1.0),
        'g_ffn1': _gain(ks[11], (DEPTH, d)),
        'w_ffn1_gate': _normal(ks[12], (DEPTH, d, D_FF), d ** -0.5),
        'w_ffn1_up': _normal(ks[13], (DEPTH, d, D_FF), d ** -0.5),
        'w_ffn1_down': _normal(ks[14], (DEPTH, D_FF, d), D_FF ** -0.5),
        'g_mix': _gain(ks[15], (DEPTH, d)),
        'w_in': _normal(ks[16], (DEPTH, d, D_IN), d ** -0.5),
        'b_igate': _normal(ks[17], (DEPTH, H_MLSTM), 0.1),
        'b_fgate': jnp.linspace(3.0, 6.0, H_MLSTM, dtype=jnp.float32)[None, :] + _normal(ks[18], (DEPTH, H_MLSTM), 0.1),
        'g_q_diff': _gain(ks[19], (DEPTH, DH_HALF)),
        'g_k_diff': _gain(ks[20], (DEPTH, DH_HALF)),
        'lambda_q1': _normal(ks[21], (DEPTH, DH_HALF), 0.1),
        'lambda_k1': _normal(ks[22], (DEPTH, DH_HALF), 0.1),
        'lambda_q2': _normal(ks[23], (DEPTH, DH_HALF), 0.1),
        'lambda_k2': _normal(ks[24], (DEPTH, DH_HALF), 0.1),
        'g_subln': _gain(ks[25], (DEPTH, HEAD_DIM)),
        'g_mlstm_out': _gain(ks[26], (DEPTH, H_MLSTM, HEAD_DIM)),
        'g_mem_in': _gain(ks[27], (DEPTH, d)),
        'w_mem_k': _normal(ks[28], (DEPTH, d, D_MEMH), d ** -0.5),
        'w_mem_v': _normal(ks[29], (DEPTH, d, D_MEMH), d ** -0.5),
        'g_q_mem': _gain(ks[30], (DEPTH, HEAD_DIM)),
        'g_k_mem': _gain(ks[31], (DEPTH, HEAD_DIM)),
        'w_out': _normal(ks[32], (DEPTH, D_MIX, d), D_MIX ** -0.5),
        'g_ffn2': _gain(ks[33], (DEPTH, d)),
        'w_ffn2_gate': _normal(ks[34], (DEPTH, d, D_FF), d ** -0.5),
        'w_ffn2_up': _normal(ks[35], (DEPTH, d, D_FF), d ** -0.5),
        'w_ffn2_down': _normal(ks[36], (DEPTH, D_FF, d), D_FF ** -0.5),
    }


def reference(x_prompt, x_sample, cache_diff_k, cache_diff_v, cache_mem_k, cache_mem_v,
              state_mlstm_C, state_mlstm_n, state_mlstm_m, page_table, mem_prompt,
              g_ffn1, w_ffn1_gate, w_ffn1_up, w_ffn1_down, g_mix, w_in, b_igate, b_fgate,
              g_q_diff, g_k_diff, lambda_q1, lambda_k1, lambda_q2, lambda_k2, g_subln,
              g_mlstm_out, g_mem_in, w_mem_k, w_mem_v, g_q_mem, g_k_mem, w_out,
              g_ffn2, w_ffn2_gate, w_ffn2_up, w_ffn2_down):
    f32 = jnp.float32
    xp = x_prompt
    xs = x_sample
    bd, t_s = xs.shape[0], xs.shape[1]
    n_past = page_table.shape[1] * cache_diff_k.shape[2]
    pos_p = jnp.arange(xp.shape[1])
    pos_s = n_past + jnp.arange(t_s)
    kp_l, vp_l, mkp_l, mvp_l, cp_l, nsp_l, mp_l = [], [], [], [], [], [], []
    ks_l, vs_l, cs_l, nss_l, ms_l = [], [], [], [], []
    for l in range(DEPTH):
        lambda_init = 0.8 - 0.6 * math.exp(-0.3 * l)
        lam = (jnp.exp(jnp.sum(lambda_q1[l].astype(f32) * lambda_k1[l].astype(f32)))
               - jnp.exp(jnp.sum(lambda_q2[l].astype(f32) * lambda_k2[l].astype(f32)))
               + lambda_init)
        ffn1 = (g_ffn1[l], w_ffn1_gate[l], w_ffn1_up[l], w_ffn1_down[l])
        ffn2 = (g_ffn2[l], w_ffn2_gate[l], w_ffn2_up[l], w_ffn2_down[l])
        proj = (w_in[l], b_igate[l], b_fgate[l], g_q_diff[l], g_k_diff[l], g_q_mem[l])
        outp = (g_subln[l], lambda_init, g_mlstm_out[l], w_out[l])

        xp = swiglu_half(xp, *ffn1)
        dq, dk, dv, lq, lk, lv, lo, ig, lfg, mq = mix_project(rms_norm(xp, g_mix[l]), pos_p, *proj)
        diff_o = diff_attn_prompt(dq, dk, dv, lam)
        h_m, (c_p, n_p, m_p) = mlstm_prompt(lq, lk, lv, ig, lfg)
        mem_k, mem_v = memory_kv(mem_prompt, g_mem_in[l], w_mem_k[l], w_mem_v[l], g_k_mem[l])
        mem_o = memory_attend(mq, mem_k, mem_v)
        xp = mix_output(xp, diff_o, h_m, lo, mem_o, *outp)
        xp = swiglu_half(xp, *ffn2)
        kp_l.append(dk)
        vp_l.append(dv)
        mkp_l.append(mem_k)
        mvp_l.append(mem_v)
        cp_l.append(c_p)
        nsp_l.append(n_p)
        mp_l.append(m_p)

        xs = swiglu_half(xs, *ffn1)
        dq, dk, dv, lq, lk, lv, lo, ig, lfg, mq = mix_project(rms_norm(xs, g_mix[l]), pos_s, *proj)
        past_k = cache_diff_k[l, page_table].reshape(bd, n_past, H_DIFF, 2, DH_HALF)
        past_v = cache_diff_v[l, page_table].reshape(bd, n_past, H_DIFF, HEAD_DIM)
        diff_o = diff_attn_sample(dq, dk, dv, past_k, past_v, lam)
        init = (state_mlstm_C[l].astype(f32), state_mlstm_n[l].astype(f32), state_mlstm_m[l].astype(f32))
        (c_s, n_s, m_s), h_m = mlstm_chunk(init, (lq, lk, lv, ig, lfg))
        mem_o = memory_attend(mq, cache_mem_k[l], cache_mem_v[l])
        xs = mix_output(xs, diff_o, h_m, lo, mem_o, *outp)
        xs = swiglu_half(xs, *ffn2)
        ks_l.append(dk)
        vs_l.append(dv)
        cs_l.append(c_s)
        nss_l.append(n_s)
        ms_l.append(m_s)

    y_prompt = xp
    y_sample = xs
    k_prompt = jnp.stack(kp_l, 0)
    v_prompt = jnp.stack(vp_l, 0)
    mem_k_prompt = jnp.stack(mkp_l, 0)
    mem_v_prompt = jnp.stack(mvp_l, 0)
    c_prompt = jnp.stack(cp_l, 0)
    n_prompt = jnp.stack(nsp_l, 0)
    m_prompt = jnp.stack(mp_l, 0)
    k_sample = jnp.stack(ks_l, 0)
    v_sample = jnp.stack(vs_l, 0)
    c_sample = jnp.stack(cs_l, 0)
    n_sample = jnp.stack(nss_l, 0)
    m_sample = jnp.stack(ms_l, 0)
    return (y_prompt, y_sample, k_prompt, v_prompt, mem_k_prompt, mem_v_prompt, c_prompt, n_prompt, m_prompt, k_sample, v_sample, c_sample, n_sample, m_sample)
```

```python
import functools
import math

import jax
import jax.numpy as jnp
from jax import lax
from jax.experimental import pallas as pl
from jax.experimental.pallas import tpu as pltpu

F32 = jnp.float32
BF16 = jnp.bfloat16

HEAD_DIM = 128
H_DIFF = 8
DH_HALF = HEAD_DIM // 2
H_MLSTM = 4
H_MEM = 4
D_DIFF = H_DIFF * HEAD_DIM
D_MLSTM = H_MLSTM * HEAD_DIM
D_MEMH = H_MEM * HEAD_DIM
ROPE_THETA = 10000.0
EPS = 1e-6
NEG = -1e30
MLSTM_CHUNK = 128

LANES = 128
GATE_LANES = LANES
MIB = 1024 * 1024

_NT = (((1,), (1,)), ((), ()))


def _params(semantics, vmem_mib):
    return pltpu.CompilerParams(dimension_semantics=semantics, vmem_limit_bytes=vmem_mib * MIB)


def _row_tile(m, pref):
    return pref if m % pref == 0 else m


def _rms(x, g):
    return x * lax.rsqrt(jnp.mean(x * x, axis=-1, keepdims=True) + EPS) * g


def _split(x):
    hi = x.astype(BF16)
    return hi, (x - hi.astype(F32)).astype(BF16)


def _mm(x, w):
    if w.dtype == BF16:
        return jnp.dot(x.astype(BF16), w, preferred_element_type=F32)
    wh, wl = _split(w)
    if x.dtype == BF16:
        return jnp.dot(x, wl, preferred_element_type=F32) + jnp.dot(x, wh, preferred_element_type=F32)
    xh, xl = _split(x)
    m = x.shape[0]
    top = jnp.dot(jnp.concatenate([xh, xl], axis=0), wh, preferred_element_type=F32)
    return (top[m:] + jnp.dot(xh, wl, preferred_element_type=F32)) + top[:m]


def _resident(shape):
    return pl.BlockSpec(shape, lambda *_: (0,) * len(shape), pipeline_mode=pl.Buffered(1))


def _log_sigmoid(x):
    return jnp.minimum(x, 0.0) - jnp.log1p(jnp.exp(-jnp.abs(x)))


def _ffn_kernel(*refs, n_ff, next_norm):
    if next_norm:
        x_ref, g_ref, wg_ref, wu_ref, wd_ref, gn_ref, y_ref, hn_ref, h_scr = refs
    else:
        x_ref, g_ref, wg_ref, wu_ref, wd_ref, y_ref, h_scr = refs
    j = pl.program_id(1)

    @pl.when(j == 0)
    def _():
        h_scr[...] = _rms(x_ref[...], g_ref[...]).astype(h_scr.dtype)

    h = h_scr[...]
    gate = _mm(h, wg_ref[...])
    up = _mm(h, wu_ref[...])
    part = _mm(gate * jax.nn.sigmoid(gate) * up, wd_ref[...])

    @pl.when(j == 0)
    def _():
        y_ref[...] = part

    @pl.when(j > 0)
    def _():
        y_ref[...] += part

    @pl.when(j == n_ff - 1)
    def _():
        y = x_ref[...] + 0.5 * y_ref[...]
        y_ref[...] = y
        if next_norm:
            hn_ref[...] = _rms(y, gn_ref[...]).astype(hn_ref.dtype)


def _ffn_half(x, g, wg, wu, wd, g_next=None):
    m, d = x.shape
    d_ff = wg.shape[1]
    tm = _row_tile(m, 512)
    tf = _row_tile(d_ff, 512)
    n_ff = d_ff // tf
    next_norm = g_next is not None
    row = pl.BlockSpec((tm, d), lambda i, j: (i, 0))
    vec = pl.BlockSpec((1, d), lambda i, j: (0, 0))
    in_specs = [row, vec, pl.BlockSpec((d, tf), lambda i, j: (0, j)), pl.BlockSpec((d, tf), lambda i, j: (0, j)),
                pl.BlockSpec((tf, d), lambda i, j: (j, 0))]
    args = [x, g.reshape(1, d), wg, wu, wd]
    out_shape = [jax.ShapeDtypeStruct((m, d), F32)]
    out_specs = [row]
    if next_norm:
        in_specs.append(vec)
        args.append(g_next.reshape(1, d))
        out_shape.append(jax.ShapeDtypeStruct((m, d), wg.dtype))
        out_specs.append(row)
    out = pl.pallas_call(
        functools.partial(_ffn_kernel, n_ff=n_ff, next_norm=next_norm),
        grid=(m // tm, n_ff),
        in_specs=in_specs,
        out_specs=out_specs,
        out_shape=out_shape,
        scratch_shapes=[pltpu.VMEM((tm, d), wg.dtype)],
        compiler_params=_params(("parallel", "arbitrary"), 48),
        name="ffn_half",
    )(*args)
    return out if next_norm else out[0]


def _diffproj_kernel(h_ref, wq_ref, wk_ref, wv_ref, gq_ref, gk_ref, cos_ref, sin_ref, seg_ref,
                     qb_ref, kf_ref, kb_ref, vf_ref, vb_ref):
    h = h_ref[...]
    tm = h.shape[0]
    heads = qb_ref.shape[1] // LANES
    cos = cos_ref[...]
    sin = sin_ref[...]
    seg = seg_ref[...]
    lane = lax.broadcasted_iota(jnp.int32, (tm, LANES), 1)
    first_half = (lane & (DH_HALF - 1)) < (DH_HALF // 2)

    def norm_rope(z, g):
        ss = _mm(z * z, seg)
        y = z * lax.rsqrt(ss * (1.0 / DH_HALF) + EPS) * g
        partner = jnp.where(first_half, pltpu.roll(y, LANES - DH_HALF // 2, axis=1), pltpu.roll(y, DH_HALF // 2, axis=1))
        return y * cos + partner * sin

    zq = _mm(h, wq_ref[...])
    gq = gq_ref[...]
    for t in range(heads):
        sl = slice(t * LANES, (t + 1) * LANES)
        qb_ref[:, sl] = (norm_rope(zq[:, sl], gq) * (DH_HALF ** -0.5)).astype(BF16)
    zk = _mm(h, wk_ref[...])
    gk = gk_ref[...]
    for t in range(heads):
        sl = slice(t * LANES, (t + 1) * LANES)
        kt = norm_rope(zk[:, sl], gk)
        kf_ref[:, sl] = kt
        kb_ref[:, sl] = kt.astype(BF16)
    zv = _mm(h, wv_ref[...])
    vf_ref[...] = zv
    vb_ref[...] = zv.astype(BF16)


def _diff_project(h, wq, wk, wv, g_q, g_k, cos, sin):
    m, d = h.shape
    tm = _row_tile(m, 512)
    tn = D_DIFF if wq.dtype == BF16 else 2 * LANES
    seg = jnp.kron(jnp.eye(2, dtype=F32), jnp.ones((DH_HALF, DH_HALF), F32)).astype(wq.dtype)
    row_in = pl.BlockSpec((tm, d), lambda i, j: (i, 0))
    w_spec = pl.BlockSpec((d, tn), lambda i, j: (0, j))
    vec = pl.BlockSpec((1, LANES), lambda i, j: (0, 0))
    tab = pl.BlockSpec((tm, LANES), lambda i, j: (i, 0))
    row_out = pl.BlockSpec((tm, tn), lambda i, j: (i, j))
    return pl.pallas_call(
        _diffproj_kernel,
        grid=(m // tm, D_DIFF // tn),
        in_specs=[row_in, w_spec, w_spec, w_spec, vec, vec, tab, tab, pl.BlockSpec((LANES, LANES), lambda i, j: (0, 0))],
        out_specs=[row_out] * 5,
        out_shape=[jax.ShapeDtypeStruct((m, D_DIFF), dt) for dt in (BF16, F32, BF16, F32, BF16)],
        compiler_params=_params(("parallel", "arbitrary"), 56),
        name="diff_project",
    )(h, wq, wk, wv, jnp.tile(g_q, 2).reshape(1, LANES), jnp.tile(g_k, 2).reshape(1, LANES), cos, sin, seg)


def _mlstmproj_kernel(h_ref, wq_ref, wk_ref, wv_ref, wo_ref, wg_ref, b_ref, q_ref, k_ref, v_ref, og_ref, gt_ref, *,
                      transpose_gates):
    h = h_ref[...]
    tm = h.shape[0]
    q_ref[...] = _mm(h, wq_ref[...]).astype(q_ref.dtype)
    k_ref[...] = (_mm(h, wk_ref[...]) * (HEAD_DIM ** -0.5)).astype(k_ref.dtype)
    v_ref[...] = _mm(h, wv_ref[...]).astype(v_ref.dtype)
    og_ref[...] = jax.nn.sigmoid(_mm(h, wo_ref[...]))
    zg = _mm(h, wg_ref[...]) + b_ref[...]
    lane = lax.broadcasted_iota(jnp.int32, (tm, GATE_LANES), 1)
    gates = jnp.where(lane < H_MLSTM, zg, _log_sigmoid(zg))
    if transpose_gates:
        for c in range(tm // MLSTM_CHUNK):
            gt = jnp.transpose(gates[c * MLSTM_CHUNK:(c + 1) * MLSTM_CHUNK, :])
            gt_ref[c] = gt[:2 * H_MLSTM, :]
    else:
        gt_ref[...] = gates


def _mlstm_project(h, w, w_gate, bias, transpose_gates):
    m, d = h.shape
    dt = w[0].dtype
    tm = _row_tile(m, 512)
    row_in = pl.BlockSpec((tm, d), lambda i: (i, 0))
    row_out = pl.BlockSpec((tm, D_MLSTM), lambda i: (i, 0))
    if transpose_gates:
        cpt = tm // MLSTM_CHUNK
        g_shape = jax.ShapeDtypeStruct((m // MLSTM_CHUNK, 2 * H_MLSTM, MLSTM_CHUNK), F32)
        g_spec = pl.BlockSpec((cpt, 2 * H_MLSTM, MLSTM_CHUNK), lambda i: (i, 0, 0))
    else:
        g_shape = jax.ShapeDtypeStruct((m, GATE_LANES), F32)
        g_spec = pl.BlockSpec((tm, GATE_LANES), lambda i: (i, 0))
    return pl.pallas_call(
        functools.partial(_mlstmproj_kernel, transpose_gates=transpose_gates),
        grid=(m // tm,),
        in_specs=[row_in] + [_resident((d, D_MLSTM))] * 4 + [_resident((d, GATE_LANES)), _resident((1, GATE_LANES))],
        out_specs=[row_out, row_out, row_out, row_out, g_spec],
        out_shape=[jax.ShapeDtypeStruct((m, D_MLSTM), t) for t in (dt, dt, dt, F32)] + [g_shape],
        compiler_params=_params(("parallel",), 48),
        name="mlstm_project",
    )(h, *w, w_gate, bias)


def _softmax_pv(s, v):
    m = jnp.max(s, axis=-1, keepdims=True)
    p = jnp.exp(s - m)
    l = jnp.sum(p, axis=-1, keepdims=True)
    return jnp.dot(p.astype(BF16), v, preferred_element_type=F32) / l


def _memq_kernel(*refs, attend):
    if attend:
        h_ref, w_ref, gq_ref, mk_ref, mv_ref, o_ref = refs
    else:
        h_ref, w_ref, gq_ref, o_ref = refs
    z = _mm(h_ref[...], w_ref[...])
    gq = gq_ref[...]
    for t in range(H_MEM):
        sl = slice(t * HEAD_DIM, (t + 1) * HEAD_DIM)
        q = _rms(z[:, sl], gq)
        if attend:
            s = lax.dot_general(q.astype(BF16), mk_ref[:, sl], _NT, preferred_element_type=F32) * (HEAD_DIM ** -0.5)
            q = _softmax_pv(s, mv_ref[:, sl])
        o_ref[:, sl] = q.astype(BF16)


def _mem_query(h, w, g_q, mem_k=None, mem_v=None):
    m, d = h.shape
    tm = _row_tile(m, 512)
    attend = mem_k is not None
    in_specs = [pl.BlockSpec((tm, d), lambda i: (i, 0)), pl.BlockSpec((d, D_MEMH), lambda i: (0, 0)),
                pl.BlockSpec((1, HEAD_DIM), lambda i: (0, 0))]
    args = [h, w, g_q.reshape(1, HEAD_DIM)]
    if attend:
        n_mem = mem_k.shape[0]
        in_specs += [pl.BlockSpec((n_mem, D_MEMH), lambda i: (0, 0))] * 2
        args += [mem_k, mem_v]
    return pl.pallas_call(
        functools.partial(_memq_kernel, attend=attend),
        grid=(m // tm,),
        in_specs=in_specs,
        out_specs=pl.BlockSpec((tm, D_MEMH), lambda i: (i, 0)),
        out_shape=jax.ShapeDtypeStruct((m, D_MEMH), BF16),
        compiler_params=_params(("parallel",), 32),
        name="mem_query",
    )(*args)


def _memkv_kernel(mem_ref, g_ref, wk_ref, wv_ref, gk_ref, kf_ref, vf_ref, kb_ref, vb_ref):
    mn = _rms(mem_ref[...], g_ref[...]).astype(BF16)
    zk = jnp.dot(mn, wk_ref[...], preferred_element_type=F32)
    gk = gk_ref[...]
    for t in range(H_MEM):
        sl = slice(t * HEAD_DIM, (t + 1) * HEAD_DIM)
        kt = _rms(zk[:, sl], gk)
        kf_ref[:, sl] = kt
        kb_ref[:, sl] = kt.astype(BF16)
    zv = jnp.dot(mn, wv_ref[...], preferred_element_type=F32)
    vf_ref[...] = zv
    vb_ref[...] = zv.astype(BF16)


def _memory_kv(mem, g_in, wk, wv, g_k):
    n_mem, d = mem.shape
    shapes = [jax.ShapeDtypeStruct((n_mem, D_MEMH), dt) for dt in (F32, F32, BF16, BF16)]
    return pl.pallas_call(
        _memkv_kernel,
        out_shape=shapes,
        compiler_params=pltpu.CompilerParams(vmem_limit_bytes=32 * MIB),
        name="memory_kv",
    )(mem, g_in.reshape(1, d), wk, wv, g_k.reshape(1, HEAD_DIM))


def _lambda_kernel(q1_ref, k1_ref, q2_ref, k2_ref, o_ref, *, lambda_init):
    a = jnp.exp(jnp.sum(q1_ref[...] * k1_ref[...], axis=-1, keepdims=True))
    b = jnp.exp(jnp.sum(q2_ref[...] * k2_ref[...], axis=-1, keepdims=True))
    o_ref[...] = a - b + lambda_init


def _diff_lambda(q1, k1, q2, k2, lambda_init):
    r = lambda a: a.reshape(1, DH_HALF).astype(F32)
    return pl.pallas_call(
        functools.partial(_lambda_kernel, lambda_init=lambda_init),
        out_shape=jax.ShapeDtypeStruct((1, 1), F32),
        name="diff_lambda",
    )(r(q1), r(k1), r(q2), r(k2))


def _split_maps(q):
    lane = lax.broadcasted_iota(jnp.int32, q.shape, 1)
    qf = q.astype(F32)
    return jnp.concatenate([jnp.where(lane < DH_HALF, qf, 0.0), jnp.where(lane >= DH_HALF, qf, 0.0)], axis=0).astype(BF16)


def _subln(d, g, out_scale):
    return _rms(d, g) * out_scale


def _attn_kernel(lam_ref, q_ref, k_ref, v_ref, gs_ref, o_ref, m_scr, l_scr, acc_scr, *, tq, out_scale):
    i = pl.program_id(1)
    qs = _split_maps(q_ref[...])
    m_scr[...] = jnp.full(m_scr.shape, NEG, F32)
    l_scr[...] = jnp.zeros(l_scr.shape, F32)
    acc_scr[...] = jnp.zeros(acc_scr.shape, F32)

    def step(j, masked):
        r0 = pl.multiple_of(j * tq, tq)
        kj = k_ref[pl.ds(r0, tq), :]
        vj = v_ref[pl.ds(r0, tq), :]
        s = lax.dot_general(qs, kj, _NT, preferred_element_type=F32)
        if masked:
            row = lax.broadcasted_iota(jnp.int32, s.shape, 0)
            col = lax.broadcasted_iota(jnp.int32, s.shape, 1)
            s = jnp.where(col <= jnp.where(row >= tq, row - tq, row), s, NEG)
        m_old = m_scr[...]
        m_new = jnp.maximum(m_old, jnp.max(s, axis=-1, keepdims=True))
        alpha = jnp.exp(m_old - m_new)
        p = jnp.exp(s - m_new)
        l_scr[...] = alpha * l_scr[...] + jnp.sum(p, axis=-1, keepdims=True)
        acc_scr[...] = alpha * acc_scr[...] + jnp.dot(p.astype(BF16), vj, preferred_element_type=F32)
        m_scr[...] = m_new

    def body(j, carry):
        step(j, False)
        return carry

    lax.fori_loop(0, i, body, 0)
    step(i, True)

    o = acc_scr[...] / l_scr[...]
    d = o[:tq] - lam_ref[0, 0] * o[tq:]
    o_ref[...] = _subln(d, gs_ref[...], out_scale).astype(BF16)


def _diff_attention_prompt(lam, q, k, v, g_subln, out_scale):
    s_len = q.shape[0]
    tq = _row_tile(s_len, 512)
    blk = pl.BlockSpec((tq, HEAD_DIM), lambda h, i: (i, h))
    full = pl.BlockSpec((s_len, HEAD_DIM), lambda h, i: (0, h))
    return pl.pallas_call(
        functools.partial(_attn_kernel, tq=tq, out_scale=out_scale),
        grid=(H_DIFF, s_len // tq),
        in_specs=[pl.BlockSpec(memory_space=pltpu.SMEM), blk, full, full, pl.BlockSpec((1, HEAD_DIM), lambda h, i: (0, 0))],
        out_specs=blk,
        out_shape=jax.ShapeDtypeStruct((s_len, D_DIFF), BF16),
        scratch_shapes=[pltpu.VMEM((2 * tq, 1), F32), pltpu.VMEM((2 * tq, 1), F32), pltpu.VMEM((2 * tq, HEAD_DIM), F32)],
        compiler_params=_params(("parallel", "arbitrary"), 40),
        name="diff_attention_prompt",
    )(lam, q, k, v, g_subln.reshape(1, HEAD_DIM))


def _decode_attn_kernel(pt_ref, lam_ref, q_ref, kn_ref, vn_ref, gs_ref, *refs, pages, out_scale):
    k_refs = refs[:pages]
    v_refs = refs[pages:2 * pages]
    o_ref, qs_scr, m_scr, l_scr, acc_scr = refs[2 * pages:]
    p_idx = pl.program_id(1)
    n_maps = 2 * H_DIFF

    @pl.when(p_idx == 0)
    def _():
        row = lax.broadcasted_iota(jnp.int32, (n_maps, D_DIFF), 0)
        col = lax.broadcasted_iota(jnp.int32, (n_maps, D_DIFF), 1)
        qb = jnp.broadcast_to(q_ref[0].astype(F32), (n_maps, D_DIFF))
        qs_scr[...] = jnp.where(col // DH_HALF == row, qb, 0.0).astype(BF16)
        m_scr[...] = jnp.full(m_scr.shape, NEG, F32)
        l_scr[...] = jnp.zeros(l_scr.shape, F32)
        acc_scr[...] = jnp.zeros(acc_scr.shape, F32)

    qs = qs_scr[...]
    s = jnp.concatenate(
        [lax.dot_general(qs, k_refs[t][0].astype(BF16), _NT, preferred_element_type=F32) for t in range(pages)], axis=1)
    m_old = m_scr[...]
    m_new = jnp.maximum(m_old, jnp.max(s, axis=-1, keepdims=True))
    alpha = jnp.exp(m_old - m_new)
    p = jnp.exp(s - m_new).astype(BF16)
    l_scr[...] = alpha * l_scr[...] + jnp.sum(p.astype(F32), axis=-1, keepdims=True)
    page = k_refs[0].shape[1]
    pv = jnp.dot(p[:, :page], v_refs[0][0].astype(BF16), preferred_element_type=F32)
    for t in range(1, pages):
        pv += jnp.dot(p[:, t * page:(t + 1) * page], v_refs[t][0].astype(BF16), preferred_element_type=F32)
    acc_scr[...] = alpha * acc_scr[...] + pv
    m_scr[...] = m_new

    @pl.when(p_idx == pl.num_programs(1) - 1)
    def _():
        s_new = jnp.sum(qs.astype(F32) * kn_ref[0], axis=-1, keepdims=True)
        m_fin = jnp.maximum(m_new, s_new)
        a_fin = jnp.exp(m_new - m_fin)
        p_new = jnp.exp(s_new - m_fin)
        l_fin = a_fin * l_scr[...] + p_new
        o = (a_fin * acc_scr[...] + p_new * vn_ref[0]) / l_fin
        lam = lam_ref[0, 0]
        gs = gs_ref[...]
        for h in range(H_DIFF):
            sl = slice(h * HEAD_DIM, (h + 1) * HEAD_DIM)
            d = o[2 * h:2 * h + 1, sl] - lam * o[2 * h + 1:2 * h + 2, sl]
            o_ref[0, :, sl] = _subln(d, gs, out_scale).astype(BF16)


def _diff_attention_decode(lam, q, k_new, v_new, cache_k, cache_v, page_table, g_subln, out_scale, pages_per_step=8):
    nb = q.shape[0]
    n_pages = page_table.shape[1]
    page = cache_k.shape[1]
    pages = pages_per_step if n_pages % pages_per_step == 0 else 1
    tok = pl.BlockSpec((1, 1, D_DIFF), lambda b, p, pt: (b, 0, 0))

    def page_spec(t):
        return pl.BlockSpec((1, page, D_DIFF), lambda b, p, pt: (pt[b, p * pages + t], 0, 0))

    grid_spec = pltpu.PrefetchScalarGridSpec(
        num_scalar_prefetch=1,
        grid=(nb, n_pages // pages),
        in_specs=[pl.BlockSpec(memory_space=pltpu.SMEM), tok, tok, tok, pl.BlockSpec((1, HEAD_DIM), lambda b, p, pt: (0, 0))]
        + [page_spec(t) for t in range(pages)] * 2,
        out_specs=tok,
        scratch_shapes=[pltpu.VMEM((2 * H_DIFF, D_DIFF), BF16), pltpu.VMEM((2 * H_DIFF, 1), F32),
                        pltpu.VMEM((2 * H_DIFF, 1), F32), pltpu.VMEM((2 * H_DIFF, D_DIFF), F32)],
    )
    out = pl.pallas_call(
        functools.partial(_decode_attn_kernel, pages=pages, out_scale=out_scale),
        grid_spec=grid_spec,
        out_shape=jax.ShapeDtypeStruct((nb, 1, D_DIFF), BF16),
        compiler_params=_params(("parallel", "arbitrary"), 40),
        name="diff_attention_decode",
    )(page_table, lam, q.reshape(nb, 1, D_DIFF), k_new.reshape(nb, 1, D_DIFF), v_new.reshape(nb, 1, D_DIFF),
      g_subln.reshape(1, HEAD_DIM), *([cache_k] * pages), *([cache_v] * pages))
    return out.reshape(nb, D_DIFF)


def _mlstm_kernel(q_ref, k_ref, v_ref, og_ref, gt_ref, go_ref, hm_ref, c_ref, n_ref, m_ref):
    @pl.when(pl.program_id(0) == 0)
    def _():
        c_ref[...] = jnp.zeros(c_ref.shape, F32)
        n_ref[...] = jnp.zeros(n_ref.shape, F32)
        m_ref[...] = jnp.zeros(m_ref.shape, F32)

    L = MLSTM_CHUNK
    lane8 = lax.broadcasted_iota(jnp.int32, (2 * H_MLSTM, L), 1)
    row = lax.broadcasted_iota(jnp.int32, (L, L), 0)
    col = lax.broadcasted_iota(jnp.int32, (L, L), 1)
    causal = col <= row

    def chunk(c, carry):
        r0 = pl.multiple_of(c * L, L)
        g8 = gt_ref[c]
        cs = g8
        d = 1
        while d < L:
            cs = cs + jnp.where(lane8 >= d, pltpu.roll(cs, d, axis=1), 0.0)
            d *= 2
        stacked = jnp.concatenate([g8, cs, jnp.zeros((L - 4 * H_MLSTM, L), F32)], axis=0)
        cols = jnp.transpose(stacked)
        for h in range(H_MLSTM):
            sl = slice(h * HEAD_DIM, (h + 1) * HEAD_DIM)
            q = q_ref[pl.ds(r0, L), sl]
            k = k_ref[pl.ds(r0, L), sl]
            v = v_ref[pl.ds(r0, L), sl]
            ig_r = g8[h:h + 1, :]
            bt_r = cs[H_MLSTM + h:H_MLSTM + h + 1, :]
            ig_c = cols[:, h:h + 1]
            bt_c = cols[:, 3 * H_MLSTM + h:3 * H_MLSTM + h + 1]
            m0 = m_ref[h:h + 1, 0:1]
            c0 = c_ref[h]
            n0 = n_ref[h:h + 1, :]

            dm = jnp.where(causal, bt_c + (ig_r - bt_r), NEG)
            inter = bt_c + m0
            m_c = jnp.maximum(inter, jnp.max(dm, axis=-1, keepdims=True))
            w_intra = jnp.exp(dm - m_c)
            w_inter = jnp.exp(inter - m_c)
            a = w_intra * lax.dot_general(q, k, _NT, preferred_element_type=F32)
            cq = lax.dot_general(q, c0.astype(BF16), _NT, preferred_element_type=F32)
            num = jnp.dot(a.astype(BF16), v, preferred_element_type=F32) + w_inter * cq
            nq = jnp.sum(q.astype(F32) * n0, axis=-1, keepdims=True)
            den = jnp.sum(a, axis=-1, keepdims=True) + w_inter * nq
            hh = num / jnp.maximum(jnp.abs(den), jnp.exp(-m_c))
            hm_ref[pl.ds(r0, L), sl] = (_rms(hh, go_ref[:, sl]) * og_ref[pl.ds(r0, L), sl]).astype(BF16)

            bl = bt_r[:, L - 1:L]
            ml = m_c[L - 1:L, :]
            w_end = jnp.exp(bl - bt_c + ig_c - ml)
            decay = jnp.exp(bl + m0 - ml)
            vw_t = jnp.transpose(w_end * v.astype(F32)).astype(BF16)
            c_ref[h] = decay * c0 + jnp.dot(vw_t, k, preferred_element_type=F32)
            n_ref[h:h + 1, :] = decay * n0 + jnp.sum(w_end * k.astype(F32), axis=0, keepdims=True)
            m_ref[h:h + 1, :] = jnp.broadcast_to(ml, (1, LANES))
        return carry

    lax.fori_loop(0, gt_ref.shape[0], chunk, 0)


def _mlstm_prompt(q, k, v, og, gates_t, g_out):
    s_len = q.shape[0]
    tm = _row_tile(s_len, 512)
    cpt = tm // MLSTM_CHUNK
    row = pl.BlockSpec((tm, D_MLSTM), lambda t: (t, 0))
    return pl.pallas_call(
        _mlstm_kernel,
        grid=(s_len // tm,),
        in_specs=[row, row, row, row, pl.BlockSpec((cpt, 2 * H_MLSTM, MLSTM_CHUNK), lambda t: (t, 0, 0)),
                  pl.BlockSpec((1, D_MLSTM), lambda t: (0, 0))],
        out_specs=[row, pl.BlockSpec((H_MLSTM, HEAD_DIM, HEAD_DIM), lambda t: (0, 0, 0)),
                   pl.BlockSpec((H_MLSTM, HEAD_DIM), lambda t: (0, 0)), pl.BlockSpec((H_MLSTM, LANES), lambda t: (0, 0))],
        out_shape=[jax.ShapeDtypeStruct((s_len, D_MLSTM), BF16), jax.ShapeDtypeStruct((H_MLSTM, HEAD_DIM, HEAD_DIM), F32),
                   jax.ShapeDtypeStruct((H_MLSTM, HEAD_DIM), F32), jax.ShapeDtypeStruct((H_MLSTM, LANES), F32)],
        compiler_params=_params(("arbitrary",), 32),
        name="mlstm_prompt",
    )(q, k, v, og, gates_t, g_out.reshape(1, D_MLSTM))


def _mlstm_step_kernel(q_ref, k_ref, v_ref, og_ref, ig_ref, lf_ref, m0_ref, n0_ref, c0_ref, go_ref,
                       hm_ref, c1_ref, n1_ref, m1_ref):
    q = q_ref[0].astype(F32)
    k = k_ref[0].astype(F32)
    v = v_ref[0].astype(F32)
    ig = ig_ref[0]
    lf = lf_ref[0]
    m0 = m0_ref[0]
    n0 = n0_ref[0]
    m1 = jnp.maximum(lf + m0, ig)
    w_i = jnp.exp(ig - m1)
    w_f = jnp.exp(lf + m0 - m1)
    a = w_i * jnp.sum(q * k, axis=-1, keepdims=True)
    cq_rows = []
    for h in range(H_MLSTM):
        qh = jnp.broadcast_to(q[h:h + 1, :], (8, HEAD_DIM)).astype(BF16)
        cq = lax.dot_general(qh, c0_ref[0, h].astype(BF16), _NT, preferred_element_type=F32)
        cq_rows.append(cq[0:1, :])
    cq = jnp.concatenate(cq_rows, axis=0)
    num = a * v + w_f * cq
    den = a + w_f * jnp.sum(n0 * q, axis=-1, keepdims=True)
    hh = num / jnp.maximum(jnp.abs(den), jnp.exp(-m1))
    hm_ref[0] = (_rms(hh, go_ref[...]) * og_ref[0]).astype(BF16)
    n1_ref[0] = w_f * n0 + w_i * k
    m1_ref[0] = m1
    wv = w_i * v
    wv_cols = jnp.transpose(jnp.concatenate([wv, jnp.zeros((HEAD_DIM - H_MLSTM, HEAD_DIM), F32)], axis=0))
    for h in range(H_MLSTM):
        c1_ref[0, h] = w_f[h:h + 1, 0:1] * c0_ref[0, h] + wv_cols[:, h:h + 1] * k[h:h + 1, :]


def _mlstm_step(q, k, v, og, ig, lf, m0, n0, c0, g_out):
    nb = q.shape[0]
    hd = (nb, H_MLSTM, HEAD_DIM)
    tok = pl.BlockSpec((1, H_MLSTM, HEAD_DIM), lambda b: (b, 0, 0))
    mat = pl.BlockSpec((1, H_MLSTM, HEAD_DIM, HEAD_DIM), lambda b: (b, 0, 0, 0))
    return pl.pallas_call(
        _mlstm_step_kernel,
        grid=(nb,),
        in_specs=[tok] * 8 + [mat, pl.BlockSpec((H_MLSTM, HEAD_DIM), lambda b: (0, 0))],
        out_specs=[tok, mat, tok, tok],
        out_shape=[jax.ShapeDtypeStruct(hd, BF16), jax.ShapeDtypeStruct((nb, H_MLSTM, HEAD_DIM, HEAD_DIM), F32),
                   jax.ShapeDtypeStruct(hd, F32), jax.ShapeDtypeStruct(hd, F32)],
        compiler_params=_params(("parallel",), 32),
        name="mlstm_step",
    )(q.reshape(hd), k.reshape(hd), v.reshape(hd), og.reshape(hd), ig, lf, m0, n0, c0, g_out)


def _mem_decode_kernel(q_ref, k_ref, v_ref, o_ref):
    for t in range(H_MEM):
        sl = slice(t * HEAD_DIM, (t + 1) * HEAD_DIM)
        q = jnp.broadcast_to(q_ref[0, :, sl], (8, HEAD_DIM))
        s = lax.dot_general(q, k_ref[0, :, sl].astype(BF16), _NT, preferred_element_type=F32) * (HEAD_DIM ** -0.5)
        o = _softmax_pv(s, v_ref[0, :, sl].astype(BF16))
        o_ref[0, :, sl] = o[0:1, :].astype(BF16)


def _memory_attend_decode(q, mem_k, mem_v):
    nb, n_mem, _ = mem_k.shape
    tok = pl.BlockSpec((1, 1, D_MEMH), lambda b: (b, 0, 0))
    mem = pl.BlockSpec((1, n_mem, D_MEMH), lambda b: (b, 0, 0))
    out = pl.pallas_call(
        _mem_decode_kernel,
        grid=(nb,),
        in_specs=[tok, mem, mem],
        out_specs=tok,
        out_shape=jax.ShapeDtypeStruct((nb, 1, D_MEMH), BF16),
        compiler_params=_params(("parallel",), 32),
        name="memory_attend_decode",
    )(q.reshape(nb, 1, D_MEMH), mem_k, mem_v)
    return out.reshape(nb, D_MEMH)


def _outproj_kernel(x_ref, d_ref, m_ref, c_ref, wd_ref, wm_ref, wc_ref, y_ref):
    y = x_ref[...] + _mm(d_ref[...], wd_ref[...])
    y += _mm(m_ref[...], wm_ref[...])
    y += _mm(c_ref[...], wc_ref[...])
    y_ref[...] = y


def _out_project(x, d, m, c, w_d, w_m, w_c):
    rows, dm = x.shape
    tm = _row_tile(rows, 512)

    def row(width):
        return pl.BlockSpec((tm, width), lambda i: (i, 0))

    return pl.pallas_call(
        _outproj_kernel,
        grid=(rows // tm,),
        in_specs=[row(dm), row(D_DIFF), row(D_MLSTM), row(D_MEMH), _resident(w_d.shape), _resident(w_m.shape),
                  _resident(w_c.shape)],
        out_specs=row(dm),
        out_shape=jax.ShapeDtypeStruct((rows, dm), F32),
        compiler_params=_params(("parallel",), 48),
        name="out_project",
    )(x, d, m, c, w_d, w_m, w_c)


def _rope_tables(pos):
    inv = 1.0 / (ROPE_THETA ** (jnp.arange(0, DH_HALF, 2, dtype=F32) / DH_HALF))
    ang = pos.astype(F32)[:, None] * inv[None, :]
    c, s = jnp.cos(ang), jnp.sin(ang)
    return jnp.tile(c, (1, 4)), jnp.concatenate([-s, s, -s, s], axis=1)


def _split_w_in(w_in, dtype):
    o = 0
    wq, wk, wv = (w_in[:, o + t * D_DIFF:o + (t + 1) * D_DIFF].astype(dtype) for t in range(3))
    o = 3 * D_DIFF
    w_ml = tuple(w_in[:, o + t * D_MLSTM:o + (t + 1) * D_MLSTM].astype(dtype) for t in range(4))
    o += 4 * D_MLSTM
    w_gate = jnp.pad(w_in[:, o:o + 2 * H_MLSTM], ((0, 0), (0, GATE_LANES - 2 * H_MLSTM))).astype(dtype)
    o += 2 * H_MLSTM
    w_mq = w_in[:, o:o + D_MEMH].astype(dtype)
    return wq, wk, wv, w_ml, w_gate, w_mq


def _split_w_out(w_out, dtype):
    w = w_out.astype(dtype)
    return w[:D_DIFF], w[D_DIFF:D_DIFF + D_MLSTM], w[D_DIFF + D_MLSTM:]


def kernel(x_prompt, x_sample, cache_diff_k, cache_diff_v, cache_mem_k, cache_mem_v, state_mlstm_C, state_mlstm_n,
           state_mlstm_m, page_table, mem_prompt, g_ffn1, w_ffn1_gate, w_ffn1_up, w_ffn1_down, g_mix, w_in, b_igate,
           b_fgate, g_q_diff, g_k_diff, lambda_q1, lambda_k1, lambda_q2, lambda_k2, g_subln, g_mlstm_out, g_mem_in,
           w_mem_k, w_mem_v, g_q_mem, g_k_mem, w_out, g_ffn2, w_ffn2_gate, w_ffn2_up, w_ffn2_down):
    depth = w_in.shape[0]
    bp, s_len, d_model = x_prompt.shape
    nb, t_s, _ = x_sample.shape
    assert bp == 1 and t_s == 1, "one prompt sequence and one new token per decode request"
    n_phys, page = cache_diff_k.shape[1], cache_diff_k.shape[2]
    n_past = page_table.shape[1] * page
    n_mem = mem_prompt.shape[1]

    xp = x_prompt.reshape(s_len, d_model)
    xs = x_sample.reshape(nb, d_model)
    cos_p, sin_p = _rope_tables(jnp.arange(s_len))
    cos_s, sin_s = (jnp.broadcast_to(t, (nb, LANES)) for t in _rope_tables(jnp.full((1,), n_past)))

    outs = [[] for _ in range(12)]
    for l in range(depth):
        lambda_init = 0.8 - 0.6 * math.exp(-0.3 * l)
        out_scale = 1.0 - lambda_init
        bf = lambda a: a[l].astype(BF16)
        ffn1 = (g_ffn1[l], bf(w_ffn1_gate), bf(w_ffn1_up), bf(w_ffn1_down))
        ffn2 = (g_ffn2[l], bf(w_ffn2_gate), bf(w_ffn2_up), bf(w_ffn2_down))
        wq, wk, wv, w_ml, w_gate, w_mq = _split_w_in(w_in[l], BF16)
        gate_bias = jnp.pad(jnp.concatenate([b_igate[l], b_fgate[l]]).astype(F32),
                            (0, GATE_LANES - 2 * H_MLSTM)).reshape(1, GATE_LANES)
        lam = _diff_lambda(lambda_q1[l], lambda_k1[l], lambda_q2[l], lambda_k2[l], lambda_init)
        g_mo = g_mlstm_out[l].reshape(H_MLSTM, HEAD_DIM)

        xp, hp = _ffn_half(xp, *ffn1, g_next=g_mix[l])
        dq, dk, dk_b, dv, dv_b = _diff_project(hp, wq, wk, wv, g_q_diff[l], g_k_diff[l], cos_p, sin_p)
        lq, lk, lv, og, gates_t = _mlstm_project(hp, w_ml, w_gate, gate_bias, transpose_gates=True)
        mk, mv, mk_b, mv_b = _memory_kv(mem_prompt.reshape(n_mem, d_model), g_mem_in[l], bf(w_mem_k), bf(w_mem_v),
                                        g_k_mem[l])
        mem_o = _mem_query(hp, w_mq, g_q_mem[l], mk_b, mv_b)
        diff_o = _diff_attention_prompt(lam, dq, dk_b, dv_b, g_subln[l], out_scale)
        h_m, c_p, n_p, m_p = _mlstm_prompt(lq, lk, lv, og, gates_t, g_mo)
        xp = _out_project(xp, diff_o, h_m, mem_o, *_split_w_out(w_out[l], BF16))
        xp = _ffn_half(xp, *ffn2)
        outs[0].append(dk.reshape(1, s_len, H_DIFF, 2, DH_HALF))
        outs[1].append(dv.reshape(1, s_len, H_DIFF, HEAD_DIM))
        outs[2].append(mk.reshape(1, n_mem, H_MEM, HEAD_DIM))
        outs[3].append(mv.reshape(1, n_mem, H_MEM, HEAD_DIM))
        outs[4].append(c_p.reshape(1, H_MLSTM, HEAD_DIM, HEAD_DIM))
        outs[5].append(n_p.reshape(1, H_MLSTM, HEAD_DIM))
        outs[6].append(m_p[:, 0].reshape(1, H_MLSTM))

        ffn1 = (g_ffn1[l], w_ffn1_gate[l], w_ffn1_up[l], w_ffn1_down[l])
        ffn2 = (g_ffn2[l], w_ffn2_gate[l], w_ffn2_up[l], w_ffn2_down[l])
        wq, wk, wv, w_ml, w_gate, w_mq = _split_w_in(w_in[l], F32)
        xs, hs = _ffn_half(xs, *ffn1, g_next=g_mix[l])
        dq, dk, _, dv, _ = _diff_project(hs, wq, wk, wv, g_q_diff[l], g_k_diff[l], cos_s, sin_s)
        lq, lk, lv, og, gates = _mlstm_project(hs, w_ml, w_gate, gate_bias, transpose_gates=False)
        mq = _mem_query(hs, w_mq, g_q_mem[l])
        diff_o = _diff_attention_decode(lam, dq, dk, dv, cache_diff_k[l].reshape(n_phys, page, D_DIFF),
                                        cache_diff_v[l].reshape(n_phys, page, D_DIFF), page_table, g_subln[l], out_scale)
        rep = lambda a: jnp.broadcast_to(a.astype(F32)[:, :, None], (nb, H_MLSTM, HEAD_DIM))
        h_m, c_s, n_s, m_s = _mlstm_step(lq, lk, lv, og, rep(gates[:, :H_MLSTM]), rep(gates[:, H_MLSTM:2 * H_MLSTM]),
                                         rep(state_mlstm_m[l]), state_mlstm_n[l].astype(F32),
                                         state_mlstm_C[l].astype(F32), g_mo)
        mem_o = _memory_attend_decode(mq, cache_mem_k[l].reshape(nb, n_mem, D_MEMH),
                                      cache_mem_v[l].reshape(nb, n_mem, D_MEMH))
        xs = _out_project(xs, diff_o, h_m.reshape(nb, D_MLSTM), mem_o, *_split_w_out(w_out[l], F32))
        xs = _ffn_half(xs, *ffn2)
        outs[7].append(dk.reshape(nb, 1, H_DIFF, 2, DH_HALF))
        outs[8].append(dv.reshape(nb, 1, H_DIFF, HEAD_DIM))
        outs[9].append(c_s)
        outs[10].append(n_s)
        outs[11].append(m_s[:, :, 0])

    stacked = [jnp.stack(o, 0) for o in outs]
    return (xp.reshape(1, s_len, d_model), xs.reshape(nb, 1, d_model), *stacked)
```

```python
import functools
import math

import jax
import jax.numpy as jnp
from jax import lax
from jax.experimental import pallas as pl
from jax.experimental.pallas import tpu as pltpu

F32 = jnp.float32
BF16 = jnp.bfloat16

HEAD_DIM = 128
H_DIFF = 8
DH_HALF = HEAD_DIM // 2
H_MLSTM = 4
H_MEM = 4
D_DIFF = H_DIFF * HEAD_DIM
D_MLSTM = H_MLSTM * HEAD_DIM
D_MEMH = H_MEM * HEAD_DIM
ROPE_THETA = 10000.0
EPS = 1e-6
NEG = -1e30
MLSTM_CHUNK = 128

LANES = 128
GATE_LANES = LANES
MIB = 1024 * 1024

_NT = (((1,), (1,)), ((), ()))


def _params(semantics, vmem_mib):
    return pltpu.CompilerParams(dimension_semantics=semantics, vmem_limit_bytes=vmem_mib * MIB)


def _row_tile(m, pref):
    return pref if m % pref == 0 else m


def _rms(x, g):
    return x * lax.rsqrt(jnp.mean(x * x, axis=-1, keepdims=True) + EPS) * g


def _split(x):
    hi = x.astype(BF16)
    return hi, (x - hi.astype(F32)).astype(BF16)


def _mm(x, w):
    if w.dtype == BF16:
        return jnp.dot(x.astype(BF16), w, preferred_element_type=F32)
    wh, wl = _split(w)
    if x.dtype == BF16:
        return jnp.dot(x, wl, preferred_element_type=F32) + jnp.dot(x, wh, preferred_element_type=F32)
    xh, xl = _split(x)
    m = x.shape[0]
    top = jnp.dot(jnp.concatenate([xh, xl], axis=0), wh, preferred_element_type=F32)
    return (top[m:] + jnp.dot(xh, wl, preferred_element_type=F32)) + top[:m]


def _resident(shape):
    return pl.BlockSpec(shape, lambda *_: (0,) * len(shape), pipeline_mode=pl.Buffered(1))


def _log_sigmoid(x):
    return jnp.minimum(x, 0.0) - jnp.log1p(jnp.exp(-jnp.abs(x)))


def _ffn_kernel(*refs, n_ff, next_norm):
    if next_norm:
        x_ref, g_ref, wg_ref, wu_ref, wd_ref, gn_ref, y_ref, hn_ref, h_scr = refs
    else:
        x_ref, g_ref, wg_ref, wu_ref, wd_ref, y_ref, h_scr = refs
    j = pl.program_id(1)

    @pl.when(j == 0)
    def _():
        h_scr[...] = _rms(x_ref[...], g_ref[...]).astype(h_scr.dtype)

    h = h_scr[...]
    gate = _mm(h, wg_ref[...])
    up = _mm(h, wu_ref[...])
    part = _mm(gate * jax.nn.sigmoid(gate) * up, wd_ref[...])

    @pl.when(j == 0)
    def _():
        y_ref[...] = part

    @pl.when(j > 0)
    def _():
        y_ref[...] += part

    @pl.when(j == n_ff - 1)
    def _():
        y = x_ref[...] + 0.5 * y_ref[...]
        y_ref[...] = y
        if next_norm:
            hn_ref[...] = _rms(y, gn_ref[...]).astype(hn_ref.dtype)


def _ffn_half(x, g, wg, wu, wd, g_next=None):
    m, d = x.shape
    d_ff = wg.shape[1]
    tm = _row_tile(m, 512)
    tf = _row_tile(d_ff, 512)
    n_ff = d_ff // tf
    next_norm = g_next is not None
    row = pl.BlockSpec((tm, d), lambda i, j: (i, 0))
    vec = pl.BlockSpec((1, d), lambda i, j: (0, 0))
    in_specs = [row, vec, pl.BlockSpec((d, tf), lambda i, j: (0, j)), pl.BlockSpec((d, tf), lambda i, j: (0, j)),
                pl.BlockSpec((tf, d), lambda i, j: (j, 0))]
    args = [x, g.reshape(1, d), wg, wu, wd]
    out_shape = [jax.ShapeDtypeStruct((m, d), F32)]
    out_specs = [row]
    if next_norm:
        in_specs.append(vec)
        args.append(g_next.reshape(1, d))
        out_shape.append(jax.ShapeDtypeStruct((m, d), wg.dtype))
        out_specs.append(row)
    out = pl.pallas_call(
        functools.partial(_ffn_kernel, n_ff=n_ff, next_norm=next_norm),
        grid=(m // tm, n_ff),
        in_specs=in_specs,
        out_specs=out_specs,
        out_shape=out_shape,
        scratch_shapes=[pltpu.VMEM((tm, d), wg.dtype)],
        compiler_params=_params(("parallel", "arbitrary"), 48),
        name="ffn_half",
    )(*args)
    return out if next_norm else out[0]


def _diffproj_kernel(*refs, q_scale, prompt):
    if prompt:
        h_ref, wq_ref, wk_ref, wv_ref, gq_ref, gk_ref, cos_ref, sin_ref, seg_ref, wvt_ref, qb_ref, kf_ref, vf_ref, kb_ref, vt_ref = refs
    else:
        h_ref, wq_ref, wk_ref, wv_ref, gq_ref, gk_ref, cos_ref, sin_ref, seg_ref, qb_ref, kf_ref, vf_ref = refs
    h = h_ref[...]
    tm = h.shape[0]
    heads = qb_ref.shape[1] // LANES
    cos = cos_ref[...]
    sin = sin_ref[...]
    seg = seg_ref[...]
    lane = lax.broadcasted_iota(jnp.int32, (tm, LANES), 1)
    first_half = (lane & (DH_HALF - 1)) < (DH_HALF // 2)

    def norm_rope(z, g):
        ss = _mm(z * z, seg)
        y = z * lax.rsqrt(ss * (1.0 / DH_HALF) + EPS) * g
        partner = jnp.where(first_half, pltpu.roll(y, LANES - DH_HALF // 2, axis=1), pltpu.roll(y, DH_HALF // 2, axis=1))
        return y * cos + partner * sin

    zq = _mm(h, wq_ref[...])
    gq = gq_ref[...]
    for t in range(heads):
        sl = slice(t * LANES, (t + 1) * LANES)
        qb_ref[:, sl] = (norm_rope(zq[:, sl], gq) * q_scale).astype(BF16)
    zk = _mm(h, wk_ref[...])
    gk = gk_ref[...]
    for t in range(heads):
        sl = slice(t * LANES, (t + 1) * LANES)
        kt = norm_rope(zk[:, sl], gk)
        kf_ref[:, sl] = kt
        if prompt:
            kb_ref[:, sl] = kt.astype(BF16)
    vf_ref[...] = _mm(h, wv_ref[...])
    if prompt:
        vt_ref[0] = lax.dot_general(wvt_ref[...], h, _NT, preferred_element_type=F32).astype(BF16)


def _diff_project(h, wq, wk, wv, g_q, g_k, cos, sin, q_scale, wv_t=None):
    m, d = h.shape
    tm = _row_tile(m, 512)
    prompt = wv_t is not None
    tn = D_DIFF if wq.dtype == BF16 else 2 * LANES
    seg = jnp.kron(jnp.eye(2, dtype=F32), jnp.ones((DH_HALF, DH_HALF), F32)).astype(wq.dtype)
    row_in = pl.BlockSpec((tm, d), lambda i, j: (i, 0))
    w_spec = _resident((d, tn)) if tn == D_DIFF else pl.BlockSpec((d, tn), lambda i, j: (0, j))
    vec = pl.BlockSpec((1, LANES), lambda i, j: (0, 0))
    tab = pl.BlockSpec((tm, LANES), lambda i, j: (i, 0))
    row_out = pl.BlockSpec((tm, tn), lambda i, j: (i, j))
    in_specs = [row_in, w_spec, w_spec, w_spec, vec, vec, tab, tab, pl.BlockSpec((LANES, LANES), lambda i, j: (0, 0))]
    args = [h, wq, wk, wv, jnp.tile(g_q, 2).reshape(1, LANES), jnp.tile(g_k, 2).reshape(1, LANES), cos, sin, seg]
    out_specs = [row_out] * 3
    out_shape = [jax.ShapeDtypeStruct((m, D_DIFF), dt) for dt in (BF16, F32, F32)]
    if prompt:
        assert tn == D_DIFF
        in_specs.append(_resident((D_DIFF, d)))
        args.append(wv_t)
        out_specs += [row_out, pl.BlockSpec((1, D_DIFF, tm), lambda i, j: (i, 0, 0))]
        out_shape += [jax.ShapeDtypeStruct((m, D_DIFF), BF16), jax.ShapeDtypeStruct((m // tm, D_DIFF, tm), BF16)]
    return pl.pallas_call(
        functools.partial(_diffproj_kernel, q_scale=q_scale, prompt=prompt),
        grid=(m // tm, D_DIFF // tn),
        in_specs=in_specs,
        out_specs=out_specs,
        out_shape=out_shape,
        compiler_params=_params(("parallel", "arbitrary"), 56),
        name="diff_project",
    )(*args)


def _mlstmproj_kernel(h_ref, wq_ref, wk_ref, wv_ref, wo_ref, wg_ref, b_ref, q_ref, k_ref, v_ref, og_ref, gt_ref, *,
                      transpose_gates):
    h = h_ref[...]
    tm = h.shape[0]
    q_ref[...] = _mm(h, wq_ref[...]).astype(q_ref.dtype)
    k_ref[...] = (_mm(h, wk_ref[...]) * (HEAD_DIM ** -0.5)).astype(k_ref.dtype)
    v_ref[...] = _mm(h, wv_ref[...]).astype(v_ref.dtype)
    og_ref[...] = jax.nn.sigmoid(_mm(h, wo_ref[...]))
    zg = _mm(h, wg_ref[...]) + b_ref[...]
    lane = lax.broadcasted_iota(jnp.int32, (tm, GATE_LANES), 1)
    gates = jnp.where(lane < H_MLSTM, zg, _log_sigmoid(zg))
    if transpose_gates:
        for c in range(tm // MLSTM_CHUNK):
            gt = jnp.transpose(gates[c * MLSTM_CHUNK:(c + 1) * MLSTM_CHUNK, :])
            gt_ref[c] = gt[:2 * H_MLSTM, :]
    else:
        gt_ref[...] = gates


def _mlstm_project(h, w, w_gate, bias, transpose_gates):
    m, d = h.shape
    dt = w[0].dtype
    tm = _row_tile(m, 512)
    row_in = pl.BlockSpec((tm, d), lambda i: (i, 0))
    row_out = pl.BlockSpec((tm, D_MLSTM), lambda i: (i, 0))
    if transpose_gates:
        cpt = tm // MLSTM_CHUNK
        g_shape = jax.ShapeDtypeStruct((m // MLSTM_CHUNK, 2 * H_MLSTM, MLSTM_CHUNK), F32)
        g_spec = pl.BlockSpec((cpt, 2 * H_MLSTM, MLSTM_CHUNK), lambda i: (i, 0, 0))
    else:
        g_shape = jax.ShapeDtypeStruct((m, GATE_LANES), F32)
        g_spec = pl.BlockSpec((tm, GATE_LANES), lambda i: (i, 0))
    return pl.pallas_call(
        functools.partial(_mlstmproj_kernel, transpose_gates=transpose_gates),
        grid=(m // tm,),
        in_specs=[row_in] + [_resident((d, D_MLSTM))] * 4 + [_resident((d, GATE_LANES)), _resident((1, GATE_LANES))],
        out_specs=[row_out, row_out, row_out, row_out, g_spec],
        out_shape=[jax.ShapeDtypeStruct((m, D_MLSTM), t) for t in (dt, dt, dt, F32)] + [g_shape],
        compiler_params=_params(("parallel",), 48),
        name="mlstm_project",
    )(h, *w, w_gate, bias)


def _softmax_pv(s, v):
    m = jnp.max(s, axis=-1, keepdims=True)
    p = jnp.exp(s - m)
    l = jnp.sum(p, axis=-1, keepdims=True)
    return jnp.dot(p.astype(BF16), v, preferred_element_type=F32) / l


def _memq_kernel(*refs, attend):
    if attend:
        h_ref, w_ref, gq_ref, mk_ref, mv_ref, o_ref = refs
    else:
        h_ref, w_ref, gq_ref, o_ref = refs
    z = _mm(h_ref[...], w_ref[...])
    gq = gq_ref[...]
    for t in range(H_MEM):
        sl = slice(t * HEAD_DIM, (t + 1) * HEAD_DIM)
        q = _rms(z[:, sl], gq)
        if attend:
            s = lax.dot_general(q.astype(BF16), mk_ref[:, sl], _NT, preferred_element_type=F32) * (HEAD_DIM ** -0.5)
            q = _softmax_pv(s, mv_ref[:, sl])
        o_ref[:, sl] = q.astype(BF16)


def _mem_query(h, w, g_q, mem_k=None, mem_v=None):
    m, d = h.shape
    tm = _row_tile(m, 512)
    attend = mem_k is not None
    in_specs = [pl.BlockSpec((tm, d), lambda i: (i, 0)), pl.BlockSpec((d, D_MEMH), lambda i: (0, 0)),
                pl.BlockSpec((1, HEAD_DIM), lambda i: (0, 0))]
    args = [h, w, g_q.reshape(1, HEAD_DIM)]
    if attend:
        n_mem = mem_k.shape[0]
        in_specs += [pl.BlockSpec((n_mem, D_MEMH), lambda i: (0, 0))] * 2
        args += [mem_k, mem_v]
    return pl.pallas_call(
        functools.partial(_memq_kernel, attend=attend),
        grid=(m // tm,),
        in_specs=in_specs,
        out_specs=pl.BlockSpec((tm, D_MEMH), lambda i: (i, 0)),
        out_shape=jax.ShapeDtypeStruct((m, D_MEMH), BF16),
        compiler_params=_params(("parallel",), 32),
        name="mem_query",
    )(*args)


def _memkv_kernel(mem_ref, g_ref, wk_ref, wv_ref, gk_ref, kf_ref, vf_ref, kb_ref, vb_ref):
    mn = _rms(mem_ref[...], g_ref[...]).astype(BF16)
    zk = jnp.dot(mn, wk_ref[...], preferred_element_type=F32)
    gk = gk_ref[...]
    for t in range(H_MEM):
        sl = slice(t * HEAD_DIM, (t + 1) * HEAD_DIM)
        kt = _rms(zk[:, sl], gk)
        kf_ref[:, sl] = kt
        kb_ref[:, sl] = kt.astype(BF16)
    zv = jnp.dot(mn, wv_ref[...], preferred_element_type=F32)
    vf_ref[...] = zv
    vb_ref[...] = zv.astype(BF16)


def _memory_kv(mem, g_in, wk, wv, g_k):
    n_mem, d = mem.shape
    shapes = [jax.ShapeDtypeStruct((n_mem, D_MEMH), dt) for dt in (F32, F32, BF16, BF16)]
    return pl.pallas_call(
        _memkv_kernel,
        out_shape=shapes,
        compiler_params=pltpu.CompilerParams(vmem_limit_bytes=32 * MIB),
        name="memory_kv",
    )(mem, g_in.reshape(1, d), wk, wv, g_k.reshape(1, HEAD_DIM))


def _lambda_kernel(q1_ref, k1_ref, q2_ref, k2_ref, o_ref, *, lambda_init):
    a = jnp.exp(jnp.sum(q1_ref[...] * k1_ref[...], axis=-1, keepdims=True))
    b = jnp.exp(jnp.sum(q2_ref[...] * k2_ref[...], axis=-1, keepdims=True))
    o_ref[...] = a - b + lambda_init


def _diff_lambda(q1, k1, q2, k2, lambda_init):
    r = lambda a: a.reshape(1, DH_HALF).astype(F32)
    return pl.pallas_call(
        functools.partial(_lambda_kernel, lambda_init=lambda_init),
        out_shape=jax.ShapeDtypeStruct((1, 1), F32),
        name="diff_lambda",
    )(r(q1), r(k1), r(q2), r(k2))


def _split_maps(q):
    lane = lax.broadcasted_iota(jnp.int32, q.shape, 1)
    qf = q.astype(F32)
    return jnp.concatenate([jnp.where(lane < DH_HALF, qf, 0.0), jnp.where(lane >= DH_HALF, qf, 0.0)], axis=0).astype(BF16)


def _subln(d, g, out_scale):
    return _rms(d, g) * out_scale


def _attn_kernel(lam_ref, q_ref, k_ref, vt_ref, gs_ref, o_ref, m_scr, l_scr, acc_scr, *, tq, out_scale):
    i = pl.program_id(1)
    qs = _split_maps(q_ref[...])
    m_scr[...] = jnp.full(m_scr.shape, NEG, F32)
    l_scr[...] = jnp.zeros(l_scr.shape, F32)
    acc_scr[...] = jnp.zeros(acc_scr.shape, F32)

    def step(j, masked):
        kj = k_ref[pl.ds(pl.multiple_of(j * tq, tq), tq), :]
        s = lax.dot_general(kj, qs, _NT, preferred_element_type=F32)
        if masked:
            key = lax.broadcasted_iota(jnp.int32, s.shape, 0)
            qry = lax.broadcasted_iota(jnp.int32, s.shape, 1)
            s = jnp.where(key <= jnp.where(qry >= tq, qry - tq, qry), s, NEG)
        m_old = m_scr[...]
        m_new = jnp.maximum(m_old, jnp.max(s, axis=0, keepdims=True))
        alpha = jnp.exp2(m_old - m_new)
        p = jnp.exp2(s - m_new)
        l_scr[...] = alpha * l_scr[...] + jnp.sum(p, axis=0, keepdims=True)
        acc_scr[...] = alpha * acc_scr[...] + jnp.dot(vt_ref[j], p.astype(BF16), preferred_element_type=F32)
        m_scr[...] = m_new

    def body(j, carry):
        step(j, False)
        return carry

    lax.fori_loop(0, i, body, 0)
    step(i, True)

    o_t = acc_scr[...] / l_scr[...]
    d = jnp.transpose(o_t[:, :tq] - lam_ref[0, 0] * o_t[:, tq:])
    o_ref[...] = _subln(d, gs_ref[...], out_scale).astype(BF16)


def _diff_attention_prompt(lam, q, k, v_t, g_subln, out_scale):
    s_len = q.shape[0]
    tq = v_t.shape[2]
    blk = pl.BlockSpec((tq, HEAD_DIM), lambda h, i: (i, h))
    return pl.pallas_call(
        functools.partial(_attn_kernel, tq=tq, out_scale=out_scale),
        grid=(H_DIFF, s_len // tq),
        in_specs=[pl.BlockSpec(memory_space=pltpu.SMEM), blk, pl.BlockSpec((s_len, HEAD_DIM), lambda h, i: (0, h)),
                  pl.BlockSpec((s_len // tq, HEAD_DIM, tq), lambda h, i: (0, h, 0)),
                  pl.BlockSpec((1, HEAD_DIM), lambda h, i: (0, 0))],
        out_specs=blk,
        out_shape=jax.ShapeDtypeStruct((s_len, D_DIFF), BF16),
        scratch_shapes=[pltpu.VMEM((1, 2 * tq), F32), pltpu.VMEM((1, 2 * tq), F32), pltpu.VMEM((HEAD_DIM, 2 * tq), F32)],
        compiler_params=_params(("parallel", "arbitrary"), 40),
        name="diff_attention_prompt",
    )(lam, q, k, v_t, g_subln.reshape(1, HEAD_DIM))


def _decode_attn_kernel(pt_ref, lam_ref, q_ref, kn_ref, vn_ref, gs_ref, *refs, pages, out_scale):
    k_refs = refs[:pages]
    v_refs = refs[pages:2 * pages]
    o_ref, qs_scr, m_scr, l_scr, acc_scr = refs[2 * pages:]
    p_idx = pl.program_id(1)
    n_maps = 2 * H_DIFF

    @pl.when(p_idx == 0)
    def _():
        row = lax.broadcasted_iota(jnp.int32, (n_maps, D_DIFF), 0)
        col = lax.broadcasted_iota(jnp.int32, (n_maps, D_DIFF), 1)
        qb = jnp.broadcast_to(q_ref[0].astype(F32), (n_maps, D_DIFF))
        qs_scr[...] = jnp.where(col // DH_HALF == row, qb, 0.0).astype(BF16)
        m_scr[...] = jnp.full(m_scr.shape, NEG, F32)
        l_scr[...] = jnp.zeros(l_scr.shape, F32)
        acc_scr[...] = jnp.zeros(acc_scr.shape, F32)

    qs = qs_scr[...]
    s = jnp.concatenate(
        [lax.dot_general(qs, k_refs[t][0], _NT, preferred_element_type=F32) for t in range(pages)], axis=1)
    m_old = m_scr[...]
    m_new = jnp.maximum(m_old, jnp.max(s, axis=-1, keepdims=True))
    alpha = jnp.exp(m_old - m_new)
    p = jnp.exp(s - m_new).astype(BF16)
    l_scr[...] = alpha * l_scr[...] + jnp.sum(p.astype(F32), axis=-1, keepdims=True)
    page = k_refs[0].shape[1]
    for h in range(H_DIFF):
        sl = slice(h * HEAD_DIM, (h + 1) * HEAD_DIM)
        rows = pl.ds(h, page, stride=H_DIFF)
        pv = jnp.dot(p[:, :page], v_refs[0][0, 0, rows, :].astype(BF16), preferred_element_type=F32)
        for t in range(1, pages):
            pv += jnp.dot(p[:, t * page:(t + 1) * page], v_refs[t][0, 0, rows, :].astype(BF16),
                          preferred_element_type=F32)
        acc_scr[:, sl] = alpha * acc_scr[:, sl] + pv
    m_scr[...] = m_new

    @pl.when(p_idx == pl.num_programs(1) - 1)
    def _():
        s_new = jnp.sum(qs.astype(F32) * kn_ref[0], axis=-1, keepdims=True)
        m_fin = jnp.maximum(m_new, s_new)
        a_fin = jnp.exp(m_new - m_fin)
        p_new = jnp.exp(s_new - m_fin)
        l_fin = a_fin * l_scr[...] + p_new
        o = (a_fin * acc_scr[...] + p_new * vn_ref[0]) / l_fin
        lam = lam_ref[0, 0]
        gs = gs_ref[...]
        for h in range(H_DIFF):
            sl = slice(h * HEAD_DIM, (h + 1) * HEAD_DIM)
            d = o[2 * h:2 * h + 1, sl] - lam * o[2 * h + 1:2 * h + 2, sl]
            o_ref[0, :, sl] = _subln(d, gs, out_scale).astype(BF16)


def _diff_attention_decode(lam, q, k_new, v_new, cache_k, cache_v, layer, page_table, g_subln, out_scale,
                           pages_per_step=8):
    nb = q.shape[0]
    n_pages = page_table.shape[1]
    page = cache_k.shape[1]
    pages = pages_per_step if n_pages % pages_per_step == 0 else 1
    tok = pl.BlockSpec((1, 1, D_DIFF), lambda b, p, pt: (b, 0, 0))

    def k_spec(t):
        return pl.BlockSpec((1, page, D_DIFF), lambda b, p, pt: (pt[b, p * pages + t], 0, 0))

    def v_spec(t):
        return pl.BlockSpec((1, 1, page * H_DIFF, HEAD_DIM), lambda b, p, pt: (layer, pt[b, p * pages + t], 0, 0))

    grid_spec = pltpu.PrefetchScalarGridSpec(
        num_scalar_prefetch=1,
        grid=(nb, n_pages // pages),
        in_specs=[pl.BlockSpec(memory_space=pltpu.SMEM), tok, tok, tok, pl.BlockSpec((1, HEAD_DIM), lambda b, p, pt: (0, 0))]
        + [k_spec(t) for t in range(pages)] + [v_spec(t) for t in range(pages)],
        out_specs=tok,
        scratch_shapes=[pltpu.VMEM((2 * H_DIFF, D_DIFF), BF16), pltpu.VMEM((2 * H_DIFF, 1), F32),
                        pltpu.VMEM((2 * H_DIFF, 1), F32), pltpu.VMEM((2 * H_DIFF, D_DIFF), F32)],
    )
    out = pl.pallas_call(
        functools.partial(_decode_attn_kernel, pages=pages, out_scale=out_scale),
        grid_spec=grid_spec,
        out_shape=jax.ShapeDtypeStruct((nb, 1, D_DIFF), BF16),
        compiler_params=_params(("parallel", "arbitrary"), 40),
        name="diff_attention_decode",
    )(page_table, lam, q.reshape(nb, 1, D_DIFF), k_new.reshape(nb, 1, D_DIFF), v_new.reshape(nb, 1, D_DIFF),
      g_subln.reshape(1, HEAD_DIM), *([cache_k] * pages), *([cache_v] * pages))
    return out.reshape(nb, D_DIFF)


def _mlstm_kernel(q_ref, k_ref, v_ref, og_ref, gt_ref, go_ref, hm_ref, c_ref, n_ref, m_ref):
    @pl.when(pl.program_id(0) == 0)
    def _():
        c_ref[...] = jnp.zeros(c_ref.shape, F32)
        n_ref[...] = jnp.zeros(n_ref.shape, F32)
        m_ref[...] = jnp.zeros(m_ref.shape, F32)

    L = MLSTM_CHUNK
    lane8 = lax.broadcasted_iota(jnp.int32, (2 * H_MLSTM, L), 1)
    row = lax.broadcasted_iota(jnp.int32, (L, L), 0)
    col = lax.broadcasted_iota(jnp.int32, (L, L), 1)
    causal = col <= row

    def chunk(c, carry):
        r0 = pl.multiple_of(c * L, L)
        g8 = gt_ref[c]
        cs = g8
        d = 1
        while d < L:
            cs = cs + jnp.where(lane8 >= d, pltpu.roll(cs, d, axis=1), 0.0)
            d *= 2
        stacked = jnp.concatenate([g8, cs, jnp.zeros((L - 4 * H_MLSTM, L), F32)], axis=0)
        cols = jnp.transpose(stacked)
        for h in range(H_MLSTM):
            sl = slice(h * HEAD_DIM, (h + 1) * HEAD_DIM)
            q = q_ref[pl.ds(r0, L), sl]
            k = k_ref[pl.ds(r0, L), sl]
            v = v_ref[pl.ds(r0, L), sl]
            ig_r = g8[h:h + 1, :]
            bt_r = cs[H_MLSTM + h:H_MLSTM + h + 1, :]
            ig_c = cols[:, h:h + 1]
            bt_c = cols[:, 3 * H_MLSTM + h:3 * H_MLSTM + h + 1]
            m0 = m_ref[h:h + 1, 0:1]
            c0 = c_ref[h]
            n0 = n_ref[h:h + 1, :]

            dm = jnp.where(causal, bt_c + (ig_r - bt_r), NEG)
            inter = bt_c + m0
            m_c = jnp.maximum(inter, jnp.max(dm, axis=-1, keepdims=True))
            w_intra = jnp.exp(dm - m_c)
            w_inter = jnp.exp(inter - m_c)
            a = w_intra * lax.dot_general(q, k, _NT, preferred_element_type=F32)
            cq = lax.dot_general(q, c0.astype(BF16), _NT, preferred_element_type=F32)
            num = jnp.dot(a.astype(BF16), v, preferred_element_type=F32) + w_inter * cq
            nq = jnp.sum(q.astype(F32) * n0, axis=-1, keepdims=True)
            den = jnp.sum(a, axis=-1, keepdims=True) + w_inter * nq
            hh = num / jnp.maximum(jnp.abs(den), jnp.exp(-m_c))
            hm_ref[pl.ds(r0, L), sl] = (_rms(hh, go_ref[:, sl]) * og_ref[pl.ds(r0, L), sl]).astype(BF16)

            bl = bt_r[:, L - 1:L]
            ml = m_c[L - 1:L, :]
            w_end = jnp.exp(bl - bt_c + ig_c - ml)
            decay = jnp.exp(bl + m0 - ml)
            vw_t = jnp.transpose(w_end * v.astype(F32)).astype(BF16)
            c_ref[h] = decay * c0 + jnp.dot(vw_t, k, preferred_element_type=F32)
            n_ref[h:h + 1, :] = decay * n0 + jnp.sum(w_end * k.astype(F32), axis=0, keepdims=True)
            m_ref[h:h + 1, :] = jnp.broadcast_to(ml, (1, LANES))
        return carry

    lax.fori_loop(0, gt_ref.shape[0], chunk, 0)


def _mlstm_prompt(q, k, v, og, gates_t, g_out):
    s_len = q.shape[0]
    tm = _row_tile(s_len, 512)
    cpt = tm // MLSTM_CHUNK
    row = pl.BlockSpec((tm, D_MLSTM), lambda t: (t, 0))
    return pl.pallas_call(
        _mlstm_kernel,
        grid=(s_len // tm,),
        in_specs=[row, row, row, row, pl.BlockSpec((cpt, 2 * H_MLSTM, MLSTM_CHUNK), lambda t: (t, 0, 0)),
                  pl.BlockSpec((1, D_MLSTM), lambda t: (0, 0))],
        out_specs=[row, pl.BlockSpec((H_MLSTM, HEAD_DIM, HEAD_DIM), lambda t: (0, 0, 0)),
                   pl.BlockSpec((H_MLSTM, HEAD_DIM), lambda t: (0, 0)), pl.BlockSpec((H_MLSTM, LANES), lambda t: (0, 0))],
        out_shape=[jax.ShapeDtypeStruct((s_len, D_MLSTM), BF16), jax.ShapeDtypeStruct((H_MLSTM, HEAD_DIM, HEAD_DIM), F32),
                   jax.ShapeDtypeStruct((H_MLSTM, HEAD_DIM), F32), jax.ShapeDtypeStruct((H_MLSTM, LANES), F32)],
        compiler_params=_params(("arbitrary",), 32),
        name="mlstm_prompt",
    )(q, k, v, og, gates_t, g_out.reshape(1, D_MLSTM))


def _mlstm_step_kernel(q_ref, k_ref, v_ref, og_ref, ig_ref, lf_ref, m0_ref, n0_ref, c0_ref, go_ref,
                       hm_ref, c1_ref, n1_ref, m1_ref):
    q = q_ref[0].astype(F32)
    k = k_ref[0].astype(F32)
    v = v_ref[0].astype(F32)
    ig = ig_ref[0]
    lf = lf_ref[0]
    m0 = m0_ref[0]
    n0 = n0_ref[0]
    m1 = jnp.maximum(lf + m0, ig)
    w_i = jnp.exp(ig - m1)
    w_f = jnp.exp(lf + m0 - m1)
    a = w_i * jnp.sum(q * k, axis=-1, keepdims=True)
    cq_rows = []
    for h in range(H_MLSTM):
        qh = jnp.broadcast_to(q[h:h + 1, :], (8, HEAD_DIM)).astype(BF16)
        cq = lax.dot_general(qh, c0_ref[0, h].astype(BF16), _NT, preferred_element_type=F32)
        cq_rows.append(cq[0:1, :])
    cq = jnp.concatenate(cq_rows, axis=0)
    num = a * v + w_f * cq
    den = a + w_f * jnp.sum(n0 * q, axis=-1, keepdims=True)
    hh = num / jnp.maximum(jnp.abs(den), jnp.exp(-m1))
    hm_ref[0] = (_rms(hh, go_ref[...]) * og_ref[0]).astype(BF16)
    n1_ref[0] = w_f * n0 + w_i * k
    m1_ref[0] = m1
    wv = w_i * v
    wv_cols = jnp.transpose(jnp.concatenate([wv, jnp.zeros((HEAD_DIM - H_MLSTM, HEAD_DIM), F32)], axis=0))
    for h in range(H_MLSTM):
        c1_ref[0, h] = w_f[h:h + 1, 0:1] * c0_ref[0, h] + wv_cols[:, h:h + 1] * k[h:h + 1, :]


def _mlstm_step(q, k, v, og, ig, lf, m0, n0, c0, g_out):
    nb = q.shape[0]
    hd = (nb, H_MLSTM, HEAD_DIM)
    tok = pl.BlockSpec((1, H_MLSTM, HEAD_DIM), lambda b: (b, 0, 0))
    mat = pl.BlockSpec((1, H_MLSTM, HEAD_DIM, HEAD_DIM), lambda b: (b, 0, 0, 0))
    return pl.pallas_call(
        _mlstm_step_kernel,
        grid=(nb,),
        in_specs=[tok] * 8 + [mat, pl.BlockSpec((H_MLSTM, HEAD_DIM), lambda b: (0, 0))],
        out_specs=[tok, mat, tok, tok],
        out_shape=[jax.ShapeDtypeStruct(hd, BF16), jax.ShapeDtypeStruct((nb, H_MLSTM, HEAD_DIM, HEAD_DIM), F32),
                   jax.ShapeDtypeStruct(hd, F32), jax.ShapeDtypeStruct(hd, F32)],
        compiler_params=_params(("parallel",), 32),
        name="mlstm_step",
    )(q.reshape(hd), k.reshape(hd), v.reshape(hd), og.reshape(hd), ig, lf, m0, n0, c0, g_out)


def _mem_decode_kernel(q_ref, k_ref, v_ref, o_ref):
    for t in range(H_MEM):
        sl = slice(t * HEAD_DIM, (t + 1) * HEAD_DIM)
        q = jnp.broadcast_to(q_ref[0, :, sl], (8, HEAD_DIM))
        s = lax.dot_general(q, k_ref[0, :, sl].astype(BF16), _NT, preferred_element_type=F32) * (HEAD_DIM ** -0.5)
        o = _softmax_pv(s, v_ref[0, :, sl].astype(BF16))
        o_ref[0, :, sl] = o[0:1, :].astype(BF16)


def _memory_attend_decode(q, mem_k, mem_v):
    nb, n_mem, _ = mem_k.shape
    tok = pl.BlockSpec((1, 1, D_MEMH), lambda b: (b, 0, 0))
    mem = pl.BlockSpec((1, n_mem, D_MEMH), lambda b: (b, 0, 0))
    out = pl.pallas_call(
        _mem_decode_kernel,
        grid=(nb,),
        in_specs=[tok, mem, mem],
        out_specs=tok,
        out_shape=jax.ShapeDtypeStruct((nb, 1, D_MEMH), BF16),
        compiler_params=_params(("parallel",), 32),
        name="memory_attend_decode",
    )(q.reshape(nb, 1, D_MEMH), mem_k, mem_v)
    return out.reshape(nb, D_MEMH)


def _outproj_kernel(x_ref, d_ref, m_ref, c_ref, wd_ref, wm_ref, wc_ref, y_ref):
    y = x_ref[...] + _mm(d_ref[...], wd_ref[...])
    y += _mm(m_ref[...], wm_ref[...])
    y += _mm(c_ref[...], wc_ref[...])
    y_ref[...] = y


def _out_project(x, d, m, c, w_d, w_m, w_c):
    rows, dm = x.shape
    tm = _row_tile(rows, 512)

    def row(width):
        return pl.BlockSpec((tm, width), lambda i: (i, 0))

    return pl.pallas_call(
        _outproj_kernel,
        grid=(rows // tm,),
        in_specs=[row(dm), row(D_DIFF), row(D_MLSTM), row(D_MEMH), _resident(w_d.shape), _resident(w_m.shape),
                  _resident(w_c.shape)],
        out_specs=row(dm),
        out_shape=jax.ShapeDtypeStruct((rows, dm), F32),
        compiler_params=_params(("parallel",), 48),
        name="out_project",
    )(x, d, m, c, w_d, w_m, w_c)


def _rope_tables(pos):
    inv = 1.0 / (ROPE_THETA ** (jnp.arange(0, DH_HALF, 2, dtype=F32) / DH_HALF))
    ang = pos.astype(F32)[:, None] * inv[None, :]
    c, s = jnp.cos(ang), jnp.sin(ang)
    return jnp.tile(c, (1, 4)), jnp.concatenate([-s, s, -s, s], axis=1)


def _split_w_in(w_in, dtype):
    o = 0
    wq, wk, wv = (w_in[:, o + t * D_DIFF:o + (t + 1) * D_DIFF].astype(dtype) for t in range(3))
    o = 3 * D_DIFF
    w_ml = tuple(w_in[:, o + t * D_MLSTM:o + (t + 1) * D_MLSTM].astype(dtype) for t in range(4))
    o += 4 * D_MLSTM
    w_gate = jnp.pad(w_in[:, o:o + 2 * H_MLSTM], ((0, 0), (0, GATE_LANES - 2 * H_MLSTM))).astype(dtype)
    o += 2 * H_MLSTM
    w_mq = w_in[:, o:o + D_MEMH].astype(dtype)
    return wq, wk, wv, w_ml, w_gate, w_mq


def _split_w_out(w_out, dtype):
    w = w_out.astype(dtype)
    return w[:D_DIFF], w[D_DIFF:D_DIFF + D_MLSTM], w[D_DIFF + D_MLSTM:]


def kernel(x_prompt, x_sample, cache_diff_k, cache_diff_v, cache_mem_k, cache_mem_v, state_mlstm_C, state_mlstm_n,
           state_mlstm_m, page_table, mem_prompt, g_ffn1, w_ffn1_gate, w_ffn1_up, w_ffn1_down, g_mix, w_in, b_igate,
           b_fgate, g_q_diff, g_k_diff, lambda_q1, lambda_k1, lambda_q2, lambda_k2, g_subln, g_mlstm_out, g_mem_in,
           w_mem_k, w_mem_v, g_q_mem, g_k_mem, w_out, g_ffn2, w_ffn2_gate, w_ffn2_up, w_ffn2_down):
    depth = w_in.shape[0]
    bp, s_len, d_model = x_prompt.shape
    nb, t_s, _ = x_sample.shape
    assert bp == 1 and t_s == 1, "one prompt sequence and one new token per decode request"
    n_phys, page = cache_diff_k.shape[1], cache_diff_k.shape[2]
    n_past = page_table.shape[1] * page
    n_mem = mem_prompt.shape[1]

    xp = x_prompt.reshape(s_len, d_model)
    xs = x_sample.reshape(nb, d_model)
    cos_p, sin_p = _rope_tables(jnp.arange(s_len))
    cos_s, sin_s = (jnp.broadcast_to(t, (nb, LANES)) for t in _rope_tables(jnp.full((1,), n_past)))

    outs = [[] for _ in range(12)]
    for l in range(depth):
        lambda_init = 0.8 - 0.6 * math.exp(-0.3 * l)
        out_scale = 1.0 - lambda_init
        bf = lambda a: a[l].astype(BF16)
        ffn1 = (g_ffn1[l], bf(w_ffn1_gate), bf(w_ffn1_up), bf(w_ffn1_down))
        ffn2 = (g_ffn2[l], bf(w_ffn2_gate), bf(w_ffn2_up), bf(w_ffn2_down))
        wq, wk, wv, w_ml, w_gate, w_mq = _split_w_in(w_in[l], BF16)
        gate_bias = jnp.pad(jnp.concatenate([b_igate[l], b_fgate[l]]).astype(F32),
                            (0, GATE_LANES - 2 * H_MLSTM)).reshape(1, GATE_LANES)
        lam = _diff_lambda(lambda_q1[l], lambda_k1[l], lambda_q2[l], lambda_k2[l], lambda_init)
        g_mo = g_mlstm_out[l].reshape(H_MLSTM, HEAD_DIM)

        xp, hp = _ffn_half(xp, *ffn1, g_next=g_mix[l])
        dq, dk, dv, dk_b, dv_t = _diff_project(hp, wq, wk, wv, g_q_diff[l], g_k_diff[l], cos_p, sin_p,
                                               DH_HALF ** -0.5 * math.log2(math.e), wv_t=wv.T)
        lq, lk, lv, og, gates_t = _mlstm_project(hp, w_ml, w_gate, gate_bias, transpose_gates=True)
        mk, mv, mk_b, mv_b = _memory_kv(mem_prompt.reshape(n_mem, d_model), g_mem_in[l], bf(w_mem_k), bf(w_mem_v),
                                        g_k_mem[l])
        mem_o = _mem_query(hp, w_mq, g_q_mem[l], mk_b, mv_b)
        diff_o = _diff_attention_prompt(lam, dq, dk_b, dv_t, g_subln[l], out_scale)
        h_m, c_p, n_p, m_p = _mlstm_prompt(lq, lk, lv, og, gates_t, g_mo)
        xp = _out_project(xp, diff_o, h_m, mem_o, *_split_w_out(w_out[l], BF16))
        xp = _ffn_half(xp, *ffn2)
        outs[0].append(dk.reshape(1, s_len, H_DIFF, 2, DH_HALF))
        outs[1].append(dv.reshape(1, s_len, H_DIFF, HEAD_DIM))
        outs[2].append(mk.reshape(1, n_mem, H_MEM, HEAD_DIM))
        outs[3].append(mv.reshape(1, n_mem, H_MEM, HEAD_DIM))
        outs[4].append(c_p.reshape(1, H_MLSTM, HEAD_DIM, HEAD_DIM))
        outs[5].append(n_p.reshape(1, H_MLSTM, HEAD_DIM))
        outs[6].append(m_p[:, 0].reshape(1, H_MLSTM))

        ffn1 = (g_ffn1[l], w_ffn1_gate[l], w_ffn1_up[l], w_ffn1_down[l])
        ffn2 = (g_ffn2[l], w_ffn2_gate[l], w_ffn2_up[l], w_ffn2_down[l])
        wq, wk, wv, w_ml, w_gate, w_mq = _split_w_in(w_in[l], F32)
        xs, hs = _ffn_half(xs, *ffn1, g_next=g_mix[l])
        dq, dk, dv = _diff_project(hs, wq, wk, wv, g_q_diff[l], g_k_diff[l], cos_s, sin_s, DH_HALF ** -0.5)
        lq, lk, lv, og, gates = _mlstm_project(hs, w_ml, w_gate, gate_bias, transpose_gates=False)
        mq = _mem_query(hs, w_mq, g_q_mem[l])
        diff_o = _diff_attention_decode(lam, dq, dk, dv, cache_diff_k[l].astype(BF16).reshape(n_phys, page, D_DIFF),
                                        cache_diff_v.reshape(depth, n_phys, page * H_DIFF, HEAD_DIM), l, page_table,
                                        g_subln[l], out_scale)
        rep = lambda a: jnp.broadcast_to(a.astype(F32)[:, :, None], (nb, H_MLSTM, HEAD_DIM))
        h_m, c_s, n_s, m_s = _mlstm_step(lq, lk, lv, og, rep(gates[:, :H_MLSTM]), rep(gates[:, H_MLSTM:2 * H_MLSTM]),
                                         rep(state_mlstm_m[l]), state_mlstm_n[l].astype(F32),
                                         state_mlstm_C[l].astype(F32), g_mo)
        mem_o = _memory_attend_decode(mq, cache_mem_k[l].reshape(nb, n_mem, D_MEMH),
                                      cache_mem_v[l].reshape(nb, n_mem, D_MEMH))
        xs = _out_project(xs, diff_o, h_m.reshape(nb, D_MLSTM), mem_o, *_split_w_out(w_out[l], F32))
        xs = _ffn_half(xs, *ffn2)
        outs[7].append(dk.reshape(nb, 1, H_DIFF, 2, DH_HALF))
        outs[8].append(dv.reshape(nb, 1, H_DIFF, HEAD_DIM))
        outs[9].append(c_s)
        outs[10].append(n_s)
        outs[11].append(m_s[:, :, 0])

    stacked = [jnp.stack(o, 0) for o in outs]
    return (xp.reshape(1, s_len, d_model), xs.reshape(nb, 1, d_model), *stacked)
```

```python
import functools
import math

import jax
import jax.numpy as jnp
from jax import lax
from jax.experimental import pallas as pl
from jax.experimental.pallas import tpu as pltpu

F32 = jnp.float32
BF16 = jnp.bfloat16

HEAD_DIM = 128
H_DIFF = 8
DH_HALF = HEAD_DIM // 2
H_MLSTM = 4
H_MEM = 4
D_DIFF = H_DIFF * HEAD_DIM
D_MLSTM = H_MLSTM * HEAD_DIM
D_MEMH = H_MEM * HEAD_DIM
ROPE_THETA = 10000.0
EPS = 1e-6
NEG = -1e30
MLSTM_CHUNK = 128

LANES = 128
GATE_LANES = LANES
MIB = 1024 * 1024

_NT = (((1,), (1,)), ((), ()))


def _params(semantics, vmem_mib):
    return pltpu.CompilerParams(dimension_semantics=semantics, vmem_limit_bytes=vmem_mib * MIB)


def _row_tile(m, pref):
    return pref if m % pref == 0 else m


def _rms(x, g):
    return x * lax.rsqrt(jnp.mean(x * x, axis=-1, keepdims=True) + EPS) * g


def _split(x):
    hi = x.astype(BF16)
    return hi, (x - hi.astype(F32)).astype(BF16)


def _mm(x, w):
    if w.dtype == BF16:
        return jnp.dot(x.astype(BF16), w, preferred_element_type=F32)
    wh, wl = _split(w)
    if x.dtype == BF16:
        return jnp.dot(x, wl, preferred_element_type=F32) + jnp.dot(x, wh, preferred_element_type=F32)
    xh, xl = _split(x)
    m = x.shape[0]
    top = jnp.dot(jnp.concatenate([xh, xl], axis=0), wh, preferred_element_type=F32)
    return (top[m:] + jnp.dot(xh, wl, preferred_element_type=F32)) + top[:m]


def _resident(shape):
    return pl.BlockSpec(shape, lambda *_: (0,) * len(shape), pipeline_mode=pl.Buffered(1))


def _log_sigmoid(x):
    return jnp.minimum(x, 0.0) - jnp.log1p(jnp.exp(-jnp.abs(x)))


def _ffn_kernel(*refs, n_ff, next_norm):
    if next_norm:
        x_ref, g_ref, wg_ref, wu_ref, wd_ref, gn_ref, y_ref, hn_ref, h_scr = refs
    else:
        x_ref, g_ref, wg_ref, wu_ref, wd_ref, y_ref, h_scr = refs
    j = pl.program_id(1)

    @pl.when(j == 0)
    def _():
        h_scr[...] = _rms(x_ref[...], g_ref[...]).astype(h_scr.dtype)

    h = h_scr[...]
    gate = _mm(h, wg_ref[...])
    up = _mm(h, wu_ref[...])
    part = _mm(gate * jax.nn.sigmoid(gate) * up, wd_ref[...])

    @pl.when(j == 0)
    def _():
        y_ref[...] = part

    @pl.when(j > 0)
    def _():
        y_ref[...] += part

    @pl.when(j == n_ff - 1)
    def _():
        y = x_ref[...] + 0.5 * y_ref[...]
        y_ref[...] = y
        if next_norm:
            hn_ref[...] = _rms(y, gn_ref[...]).astype(hn_ref.dtype)


def _ffn_half(x, g, wg, wu, wd, g_next=None):
    m, d = x.shape
    d_ff = wg.shape[1]
    tm = _row_tile(m, 512)
    tf = _row_tile(d_ff, 512)
    n_ff = d_ff // tf
    next_norm = g_next is not None
    row = pl.BlockSpec((tm, d), lambda i, j: (i, 0))
    vec = pl.BlockSpec((1, d), lambda i, j: (0, 0))
    in_specs = [row, vec, pl.BlockSpec((d, tf), lambda i, j: (0, j)), pl.BlockSpec((d, tf), lambda i, j: (0, j)),
                pl.BlockSpec((tf, d), lambda i, j: (j, 0))]
    args = [x, g.reshape(1, d), wg, wu, wd]
    out_shape = [jax.ShapeDtypeStruct((m, d), F32)]
    out_specs = [row]
    if next_norm:
        in_specs.append(vec)
        args.append(g_next.reshape(1, d))
        out_shape.append(jax.ShapeDtypeStruct((m, d), wg.dtype))
        out_specs.append(row)
    out = pl.pallas_call(
        functools.partial(_ffn_kernel, n_ff=n_ff, next_norm=next_norm),
        grid=(m // tm, n_ff),
        in_specs=in_specs,
        out_specs=out_specs,
        out_shape=out_shape,
        scratch_shapes=[pltpu.VMEM((tm, d), wg.dtype)],
        compiler_params=_params(("parallel", "arbitrary"), 48),
        name="ffn_half",
    )(*args)
    return out if next_norm else out[0]


def _diffproj_kernel(*refs, q_scale, prompt):
    if prompt:
        h_ref, wq_ref, wk_ref, wv_ref, gq_ref, gk_ref, cos_ref, sin_ref, seg_ref, wvt_ref, qb_ref, kf_ref, vf_ref, kb_ref, vt_ref = refs
    else:
        h_ref, wq_ref, wk_ref, wv_ref, gq_ref, gk_ref, cos_ref, sin_ref, seg_ref, qb_ref, kf_ref, vf_ref = refs
    h = h_ref[...]
    tm = h.shape[0]
    heads = qb_ref.shape[1] // LANES
    cos = cos_ref[...]
    sin = sin_ref[...]
    seg = seg_ref[...]
    lane = lax.broadcasted_iota(jnp.int32, (tm, LANES), 1)
    first_half = (lane & (DH_HALF - 1)) < (DH_HALF // 2)

    def norm_rope(z, g):
        ss = _mm(z * z, seg)
        y = z * lax.rsqrt(ss * (1.0 / DH_HALF) + EPS) * g
        partner = jnp.where(first_half, pltpu.roll(y, LANES - DH_HALF // 2, axis=1), pltpu.roll(y, DH_HALF // 2, axis=1))
        return y * cos + partner * sin

    zq = _mm(h, wq_ref[...])
    gq = gq_ref[...]
    for t in range(heads):
        sl = slice(t * LANES, (t + 1) * LANES)
        qb_ref[:, sl] = (norm_rope(zq[:, sl], gq) * q_scale).astype(BF16)
    zk = _mm(h, wk_ref[...])
    gk = gk_ref[...]
    for t in range(heads):
        sl = slice(t * LANES, (t + 1) * LANES)
        kt = norm_rope(zk[:, sl], gk)
        kf_ref[:, sl] = kt
        if prompt:
            kb_ref[:, sl] = kt.astype(BF16)
    vf_ref[...] = _mm(h, wv_ref[...])
    if prompt:
        vt_ref[0] = lax.dot_general(wvt_ref[...], h, _NT, preferred_element_type=F32).astype(BF16)


def _diff_project(h, wq, wk, wv, g_q, g_k, cos, sin, q_scale, wv_t=None):
    m, d = h.shape
    tm = _row_tile(m, 512)
    prompt = wv_t is not None
    tn = D_DIFF if wq.dtype == BF16 else 2 * LANES
    seg = jnp.kron(jnp.eye(2, dtype=F32), jnp.ones((DH_HALF, DH_HALF), F32)).astype(wq.dtype)
    row_in = pl.BlockSpec((tm, d), lambda i, j: (i, 0))
    w_spec = _resident((d, tn)) if tn == D_DIFF else pl.BlockSpec((d, tn), lambda i, j: (0, j))
    vec = pl.BlockSpec((1, LANES), lambda i, j: (0, 0))
    tab = pl.BlockSpec((tm, LANES), lambda i, j: (i, 0))
    row_out = pl.BlockSpec((tm, tn), lambda i, j: (i, j))
    in_specs = [row_in, w_spec, w_spec, w_spec, vec, vec, tab, tab, pl.BlockSpec((LANES, LANES), lambda i, j: (0, 0))]
    args = [h, wq, wk, wv, jnp.tile(g_q, 2).reshape(1, LANES), jnp.tile(g_k, 2).reshape(1, LANES), cos, sin, seg]
    out_specs = [row_out] * 3
    out_shape = [jax.ShapeDtypeStruct((m, D_DIFF), dt) for dt in (BF16, F32, F32)]
    if prompt:
        assert tn == D_DIFF
        in_specs.append(_resident((D_DIFF, d)))
        args.append(wv_t)
        out_specs += [row_out, pl.BlockSpec((1, D_DIFF, tm), lambda i, j: (i, 0, 0))]
        out_shape += [jax.ShapeDtypeStruct((m, D_DIFF), BF16), jax.ShapeDtypeStruct((m // tm, D_DIFF, tm), BF16)]
    return pl.pallas_call(
        functools.partial(_diffproj_kernel, q_scale=q_scale, prompt=prompt),
        grid=(m // tm, D_DIFF // tn),
        in_specs=in_specs,
        out_specs=out_specs,
        out_shape=out_shape,
        compiler_params=_params(("parallel", "arbitrary"), 56),
        name="diff_project",
    )(*args)


def _mlstmproj_kernel(h_ref, wq_ref, wk_ref, wv_ref, wo_ref, wg_ref, b_ref, q_ref, k_ref, v_ref, og_ref, gt_ref, *,
                      transpose_gates):
    h = h_ref[...]
    tm = h.shape[0]
    q_ref[...] = _mm(h, wq_ref[...]).astype(q_ref.dtype)
    k_ref[...] = (_mm(h, wk_ref[...]) * (HEAD_DIM ** -0.5)).astype(k_ref.dtype)
    v_ref[...] = _mm(h, wv_ref[...]).astype(v_ref.dtype)
    og_ref[...] = jax.nn.sigmoid(_mm(h, wo_ref[...]))
    zg = _mm(h, wg_ref[...]) + b_ref[...]
    lane = lax.broadcasted_iota(jnp.int32, (tm, GATE_LANES), 1)
    gates = jnp.where(lane < H_MLSTM, zg, _log_sigmoid(zg))
    if transpose_gates:
        for c in range(tm // MLSTM_CHUNK):
            gt = jnp.transpose(gates[c * MLSTM_CHUNK:(c + 1) * MLSTM_CHUNK, :])
            gt_ref[c] = gt[:2 * H_MLSTM, :]
    else:
        gt_ref[...] = gates


def _mlstm_project(h, w, w_gate, bias, transpose_gates):
    m, d = h.shape
    dt = w[0].dtype
    tm = _row_tile(m, 512)
    row_in = pl.BlockSpec((tm, d), lambda i: (i, 0))
    row_out = pl.BlockSpec((tm, D_MLSTM), lambda i: (i, 0))
    if transpose_gates:
        cpt = tm // MLSTM_CHUNK
        g_shape = jax.ShapeDtypeStruct((m // MLSTM_CHUNK, 2 * H_MLSTM, MLSTM_CHUNK), F32)
        g_spec = pl.BlockSpec((cpt, 2 * H_MLSTM, MLSTM_CHUNK), lambda i: (i, 0, 0))
    else:
        g_shape = jax.ShapeDtypeStruct((m, GATE_LANES), F32)
        g_spec = pl.BlockSpec((tm, GATE_LANES), lambda i: (i, 0))
    return pl.pallas_call(
        functools.partial(_mlstmproj_kernel, transpose_gates=transpose_gates),
        grid=(m // tm,),
        in_specs=[row_in] + [_resident((d, D_MLSTM))] * 4 + [_resident((d, GATE_LANES)), _resident((1, GATE_LANES))],
        out_specs=[row_out, row_out, row_out, row_out, g_spec],
        out_shape=[jax.ShapeDtypeStruct((m, D_MLSTM), t) for t in (dt, dt, dt, F32)] + [g_shape],
        compiler_params=_params(("parallel",), 48),
        name="mlstm_project",
    )(h, *w, w_gate, bias)


def _softmax_pv(s, v):
    m = jnp.max(s, axis=-1, keepdims=True)
    p = jnp.exp(s - m)
    l = jnp.sum(p, axis=-1, keepdims=True)
    return jnp.dot(p.astype(BF16), v, preferred_element_type=F32) / l


def _memq_kernel(*refs, attend):
    if attend:
        h_ref, w_ref, gq_ref, mk_ref, mv_ref, o_ref = refs
    else:
        h_ref, w_ref, gq_ref, o_ref = refs
    z = _mm(h_ref[...], w_ref[...])
    gq = gq_ref[...]
    for t in range(H_MEM):
        sl = slice(t * HEAD_DIM, (t + 1) * HEAD_DIM)
        q = _rms(z[:, sl], gq)
        if attend:
            s = lax.dot_general(q.astype(BF16), mk_ref[:, sl], _NT, preferred_element_type=F32) * (HEAD_DIM ** -0.5)
            q = _softmax_pv(s, mv_ref[:, sl])
        o_ref[:, sl] = q.astype(BF16)


def _mem_query(h, w, g_q, mem_k=None, mem_v=None):
    m, d = h.shape
    tm = _row_tile(m, 512)
    attend = mem_k is not None
    in_specs = [pl.BlockSpec((tm, d), lambda i: (i, 0)), pl.BlockSpec((d, D_MEMH), lambda i: (0, 0)),
                pl.BlockSpec((1, HEAD_DIM), lambda i: (0, 0))]
    args = [h, w, g_q.reshape(1, HEAD_DIM)]
    if attend:
        n_mem = mem_k.shape[0]
        in_specs += [pl.BlockSpec((n_mem, D_MEMH), lambda i: (0, 0))] * 2
        args += [mem_k, mem_v]
    return pl.pallas_call(
        functools.partial(_memq_kernel, attend=attend),
        grid=(m // tm,),
        in_specs=in_specs,
        out_specs=pl.BlockSpec((tm, D_MEMH), lambda i: (i, 0)),
        out_shape=jax.ShapeDtypeStruct((m, D_MEMH), BF16),
        compiler_params=_params(("parallel",), 32),
        name="mem_query",
    )(*args)


def _memkv_kernel(mem_ref, g_ref, wk_ref, wv_ref, gk_ref, kf_ref, vf_ref, kb_ref, vb_ref):
    mn = _rms(mem_ref[...], g_ref[...]).astype(BF16)
    zk = jnp.dot(mn, wk_ref[...], preferred_element_type=F32)
    gk = gk_ref[...]
    for t in range(H_MEM):
        sl = slice(t * HEAD_DIM, (t + 1) * HEAD_DIM)
        kt = _rms(zk[:, sl], gk)
        kf_ref[:, sl] = kt
        kb_ref[:, sl] = kt.astype(BF16)
    zv = jnp.dot(mn, wv_ref[...], preferred_element_type=F32)
    vf_ref[...] = zv
    vb_ref[...] = zv.astype(BF16)


def _memory_kv(mem, g_in, wk, wv, g_k):
    n_mem, d = mem.shape
    shapes = [jax.ShapeDtypeStruct((n_mem, D_MEMH), dt) for dt in (F32, F32, BF16, BF16)]
    return pl.pallas_call(
        _memkv_kernel,
        out_shape=shapes,
        compiler_params=pltpu.CompilerParams(vmem_limit_bytes=32 * MIB),
        name="memory_kv",
    )(mem, g_in.reshape(1, d), wk, wv, g_k.reshape(1, HEAD_DIM))


def _lambda_kernel(q1_ref, k1_ref, q2_ref, k2_ref, o_ref, *, lambda_init):
    a = jnp.exp(jnp.sum(q1_ref[...] * k1_ref[...], axis=-1, keepdims=True))
    b = jnp.exp(jnp.sum(q2_ref[...] * k2_ref[...], axis=-1, keepdims=True))
    o_ref[...] = a - b + lambda_init


def _diff_lambda(q1, k1, q2, k2, lambda_init):
    r = lambda a: a.reshape(1, DH_HALF).astype(F32)
    return pl.pallas_call(
        functools.partial(_lambda_kernel, lambda_init=lambda_init),
        out_shape=jax.ShapeDtypeStruct((1, 1), F32),
        name="diff_lambda",
    )(r(q1), r(k1), r(q2), r(k2))


def _split_maps(q):
    lane = lax.broadcasted_iota(jnp.int32, q.shape, 1)
    qf = q.astype(F32)
    return jnp.concatenate([jnp.where(lane < DH_HALF, qf, 0.0), jnp.where(lane >= DH_HALF, qf, 0.0)], axis=0).astype(BF16)


def _subln(d, g, out_scale):
    return _rms(d, g) * out_scale


def _attn_kernel(lam_ref, q_ref, k_ref, vt_ref, gs_ref, o_ref, m_scr, l_scr, acc_scr, *, tq, out_scale):
    i = pl.program_id(1)
    qs = _split_maps(q_ref[...])
    m_scr[...] = jnp.full(m_scr.shape, NEG, F32)
    l_scr[...] = jnp.zeros(l_scr.shape, F32)
    acc_scr[...] = jnp.zeros(acc_scr.shape, F32)

    def step(j, masked):
        kj = k_ref[pl.ds(pl.multiple_of(j * tq, tq), tq), :]
        s = lax.dot_general(kj, qs, _NT, preferred_element_type=F32)
        if masked:
            key = lax.broadcasted_iota(jnp.int32, s.shape, 0)
            qry = lax.broadcasted_iota(jnp.int32, s.shape, 1)
            s = jnp.where(key <= jnp.where(qry >= tq, qry - tq, qry), s, NEG)
        m_old = m_scr[...]
        m_new = jnp.maximum(m_old, jnp.max(s, axis=0, keepdims=True))
        alpha = jnp.exp2(m_old - m_new)
        p = jnp.exp2(s - m_new)
        l_scr[...] = alpha * l_scr[...] + jnp.sum(p, axis=0, keepdims=True)
        pb = p.astype(BF16)
        tv = vt_ref.shape[2]
        pv = jnp.dot(vt_ref[j * (tq // tv)], pb[:tv], preferred_element_type=F32)
        for u in range(1, tq // tv):
            pv += jnp.dot(vt_ref[j * (tq // tv) + u], pb[u * tv:(u + 1) * tv], preferred_element_type=F32)
        acc_scr[...] = alpha * acc_scr[...] + pv
        m_scr[...] = m_new

    def body(j, carry):
        step(j, False)
        return carry

    lax.fori_loop(0, i, body, 0)
    step(i, True)

    o_t = acc_scr[...] / l_scr[...]
    d = jnp.transpose(o_t[:, :tq] - lam_ref[0, 0] * o_t[:, tq:])
    o_ref[...] = _subln(d, gs_ref[...], out_scale).astype(BF16)


def _diff_attention_prompt(lam, q, k, v_t, g_subln, out_scale):
    s_len = q.shape[0]
    tv = v_t.shape[2]
    tq = _row_tile(s_len, 1024)
    assert tq % tv == 0
    blk = pl.BlockSpec((tq, HEAD_DIM), lambda h, i: (i, h))
    return pl.pallas_call(
        functools.partial(_attn_kernel, tq=tq, out_scale=out_scale),
        grid=(H_DIFF, s_len // tq),
        in_specs=[pl.BlockSpec(memory_space=pltpu.SMEM), blk, pl.BlockSpec((s_len, HEAD_DIM), lambda h, i: (0, h)),
                  pl.BlockSpec((s_len // tv, HEAD_DIM, tv), lambda h, i: (0, h, 0)),
                  pl.BlockSpec((1, HEAD_DIM), lambda h, i: (0, 0))],
        out_specs=blk,
        out_shape=jax.ShapeDtypeStruct((s_len, D_DIFF), BF16),
        scratch_shapes=[pltpu.VMEM((1, 2 * tq), F32), pltpu.VMEM((1, 2 * tq), F32), pltpu.VMEM((HEAD_DIM, 2 * tq), F32)],
        compiler_params=_params(("parallel", "arbitrary"), 48),
        name="diff_attention_prompt",
    )(lam, q, k, v_t, g_subln.reshape(1, HEAD_DIM))


def _decode_attn_kernel(pt_ref, lam_ref, q_ref, kn_ref, vn_ref, gs_ref, *refs, pages, out_scale):
    k_refs = refs[:pages]
    v_refs = refs[pages:2 * pages]
    o_ref, m_scr, l_scr, acc_scr = refs[2 * pages:]
    p_idx = pl.program_id(1)
    n_maps = 2 * H_DIFF
    page = v_refs[0].shape[2] // H_DIFF

    @pl.when(p_idx == 0)
    def _():
        m_scr[...] = jnp.full(m_scr.shape, NEG, F32)
        l_scr[...] = jnp.zeros(l_scr.shape, F32)
        acc_scr[...] = jnp.zeros(acc_scr.shape, F32)

    q16 = q_ref[0]
    map_id = lax.broadcasted_iota(jnp.int32, (n_maps, page), 0)
    s_pages = []
    for t in range(pages):
        s_t = jnp.zeros((n_maps, page), F32)
        for r in range(n_maps):
            k_r = k_refs[t][0, 0, pl.ds(r, page, stride=n_maps), :].astype(BF16)
            s_t = jnp.where(map_id == r, lax.dot_general(q16, k_r, _NT, preferred_element_type=F32), s_t)
        s_pages.append(s_t)
    s = jnp.concatenate(s_pages, axis=1)
    m_old = m_scr[...]
    m_new = jnp.maximum(m_old, jnp.max(s, axis=-1, keepdims=True))
    alpha = jnp.exp(m_old - m_new)
    p = jnp.exp(s - m_new).astype(BF16)
    l_scr[...] = alpha * l_scr[...] + jnp.sum(p.astype(F32), axis=-1, keepdims=True)
    for h in range(H_DIFF):
        sl = slice(h * HEAD_DIM, (h + 1) * HEAD_DIM)
        rows = pl.ds(h, page, stride=H_DIFF)
        pv = jnp.dot(p[:, :page], v_refs[0][0, 0, rows, :].astype(BF16), preferred_element_type=F32)
        for t in range(1, pages):
            pv += jnp.dot(p[:, t * page:(t + 1) * page], v_refs[t][0, 0, rows, :].astype(BF16),
                          preferred_element_type=F32)
        acc_scr[:, sl] = alpha * acc_scr[:, sl] + pv
    m_scr[...] = m_new

    @pl.when(p_idx == pl.num_programs(1) - 1)
    def _():
        s_new = jnp.sum(q16.astype(F32) * kn_ref[0], axis=-1, keepdims=True)
        m_fin = jnp.maximum(m_new, s_new)
        a_fin = jnp.exp(m_new - m_fin)
        p_new = jnp.exp(s_new - m_fin)
        l_fin = a_fin * l_scr[...] + p_new
        o = (a_fin * acc_scr[...] + p_new * vn_ref[0]) / l_fin
        lam = lam_ref[0, 0]
        gs = gs_ref[...]
        for h in range(H_DIFF):
            sl = slice(h * HEAD_DIM, (h + 1) * HEAD_DIM)
            d = o[2 * h:2 * h + 1, sl] - lam * o[2 * h + 1:2 * h + 2, sl]
            o_ref[0, :, sl] = _subln(d, gs, out_scale).astype(BF16)


def _diff_attention_decode(lam, q, k_new, v_new, cache_k, cache_v, layer, page_table, g_subln, out_scale,
                           pages_per_step=8):
    nb = q.shape[0]
    n_pages = page_table.shape[1]
    n_maps = 2 * H_DIFF
    page = cache_v.shape[2] // H_DIFF
    pages = pages_per_step if n_pages % pages_per_step == 0 else 1
    tok = pl.BlockSpec((1, 1, D_DIFF), lambda b, p, pt: (b, 0, 0))
    maps = pl.BlockSpec((1, n_maps, DH_HALF), lambda b, p, pt: (b, 0, 0))

    def k_spec(t):
        return pl.BlockSpec((1, 1, page * n_maps, DH_HALF), lambda b, p, pt: (layer, pt[b, p * pages + t], 0, 0))

    def v_spec(t):
        return pl.BlockSpec((1, 1, page * H_DIFF, HEAD_DIM), lambda b, p, pt: (layer, pt[b, p * pages + t], 0, 0))

    grid_spec = pltpu.PrefetchScalarGridSpec(
        num_scalar_prefetch=1,
        grid=(nb, n_pages // pages),
        in_specs=[pl.BlockSpec(memory_space=pltpu.SMEM), maps, maps, tok, pl.BlockSpec((1, HEAD_DIM), lambda b, p, pt: (0, 0))]
        + [k_spec(t) for t in range(pages)] + [v_spec(t) for t in range(pages)],
        out_specs=tok,
        scratch_shapes=[pltpu.VMEM((n_maps, 1), F32), pltpu.VMEM((n_maps, 1), F32), pltpu.VMEM((n_maps, D_DIFF), F32)],
    )
    out = pl.pallas_call(
        functools.partial(_decode_attn_kernel, pages=pages, out_scale=out_scale),
        grid_spec=grid_spec,
        out_shape=jax.ShapeDtypeStruct((nb, 1, D_DIFF), BF16),
        compiler_params=_params(("parallel", "arbitrary"), 48),
        name="diff_attention_decode",
    )(page_table, lam, q.reshape(nb, n_maps, DH_HALF), k_new.reshape(nb, n_maps, DH_HALF), v_new.reshape(nb, 1, D_DIFF),
      g_subln.reshape(1, HEAD_DIM), *([cache_k] * pages), *([cache_v] * pages))
    return out.reshape(nb, D_DIFF)


def _mlstm_kernel(q_ref, k_ref, v_ref, og_ref, gt_ref, go_ref, hm_ref, c_ref, n_ref, m_ref):
    @pl.when(pl.program_id(0) == 0)
    def _():
        c_ref[...] = jnp.zeros(c_ref.shape, F32)
        n_ref[...] = jnp.zeros(n_ref.shape, F32)
        m_ref[...] = jnp.zeros(m_ref.shape, F32)

    L = MLSTM_CHUNK
    lane8 = lax.broadcasted_iota(jnp.int32, (2 * H_MLSTM, L), 1)
    row = lax.broadcasted_iota(jnp.int32, (L, L), 0)
    col = lax.broadcasted_iota(jnp.int32, (L, L), 1)
    causal = col <= row

    def chunk(c, carry):
        r0 = pl.multiple_of(c * L, L)
        g8 = gt_ref[c]
        cs = g8
        d = 1
        while d < L:
            cs = cs + jnp.where(lane8 >= d, pltpu.roll(cs, d, axis=1), 0.0)
            d *= 2
        stacked = jnp.concatenate([g8, cs, jnp.zeros((L - 4 * H_MLSTM, L), F32)], axis=0)
        cols = jnp.transpose(stacked)
        for h in range(H_MLSTM):
            sl = slice(h * HEAD_DIM, (h + 1) * HEAD_DIM)
            q = q_ref[pl.ds(r0, L), sl]
            k = k_ref[pl.ds(r0, L), sl]
            v = v_ref[pl.ds(r0, L), sl]
            ig_r = g8[h:h + 1, :]
            bt_r = cs[H_MLSTM + h:H_MLSTM + h + 1, :]
            ig_c = cols[:, h:h + 1]
            bt_c = cols[:, 3 * H_MLSTM + h:3 * H_MLSTM + h + 1]
            m0 = m_ref[h:h + 1, 0:1]
            c0 = c_ref[h]
            n0 = n_ref[h:h + 1, :]

            dm = jnp.where(causal, bt_c + (ig_r - bt_r), NEG)
            inter = bt_c + m0
            m_c = jnp.maximum(inter, jnp.max(dm, axis=-1, keepdims=True))
            w_intra = jnp.exp(dm - m_c)
            w_inter = jnp.exp(inter - m_c)
            a = w_intra * lax.dot_general(q, k, _NT, preferred_element_type=F32)
            cq = lax.dot_general(q, c0.astype(BF16), _NT, preferred_element_type=F32)
            num = jnp.dot(a.astype(BF16), v, preferred_element_type=F32) + w_inter * cq
            nq = jnp.sum(q.astype(F32) * n0, axis=-1, keepdims=True)
            den = jnp.sum(a, axis=-1, keepdims=True) + w_inter * nq
            hh = num / jnp.maximum(jnp.abs(den), jnp.exp(-m_c))
            hm_ref[pl.ds(r0, L), sl] = (_rms(hh, go_ref[:, sl]) * og_ref[pl.ds(r0, L), sl]).astype(BF16)

            bl = bt_r[:, L - 1:L]
            ml = m_c[L - 1:L, :]
            w_end = jnp.exp(bl - bt_c + ig_c - ml)
            decay = jnp.exp(bl + m0 - ml)
            vw_t = jnp.transpose(w_end * v.astype(F32)).astype(BF16)
            c_ref[h] = decay * c0 + jnp.dot(vw_t, k, preferred_element_type=F32)
            n_ref[h:h + 1, :] = decay * n0 + jnp.sum(w_end * k.astype(F32), axis=0, keepdims=True)
            m_ref[h:h + 1, :] = jnp.broadcast_to(ml, (1, LANES))
        return carry

    lax.fori_loop(0, gt_ref.shape[0], chunk, 0)


def _mlstm_prompt(q, k, v, og, gates_t, g_out):
    s_len = q.shape[0]
    tm = _row_tile(s_len, 512)
    cpt = tm // MLSTM_CHUNK
    row = pl.BlockSpec((tm, D_MLSTM), lambda t: (t, 0))
    return pl.pallas_call(
        _mlstm_kernel,
        grid=(s_len // tm,),
        in_specs=[row, row, row, row, pl.BlockSpec((cpt, 2 * H_MLSTM, MLSTM_CHUNK), lambda t: (t, 0, 0)),
                  pl.BlockSpec((1, D_MLSTM), lambda t: (0, 0))],
        out_specs=[row, pl.BlockSpec((H_MLSTM, HEAD_DIM, HEAD_DIM), lambda t: (0, 0, 0)),
                   pl.BlockSpec((H_MLSTM, HEAD_DIM), lambda t: (0, 0)), pl.BlockSpec((H_MLSTM, LANES), lambda t: (0, 0))],
        out_shape=[jax.ShapeDtypeStruct((s_len, D_MLSTM), BF16), jax.ShapeDtypeStruct((H_MLSTM, HEAD_DIM, HEAD_DIM), F32),
                   jax.ShapeDtypeStruct((H_MLSTM, HEAD_DIM), F32), jax.ShapeDtypeStruct((H_MLSTM, LANES), F32)],
        compiler_params=_params(("arbitrary",), 32),
        name="mlstm_prompt",
    )(q, k, v, og, gates_t, g_out.reshape(1, D_MLSTM))


def _mlstm_step_kernel(q_ref, k_ref, v_ref, og_ref, ig_ref, lf_ref, m0_ref, n0_ref, c0_ref, go_ref,
                       hm_ref, c1_ref, n1_ref, m1_ref):
    q = q_ref[0].astype(F32)
    k = k_ref[0].astype(F32)
    v = v_ref[0].astype(F32)
    ig = ig_ref[0]
    lf = lf_ref[0]
    m0 = m0_ref[0]
    n0 = n0_ref[0]
    m1 = jnp.maximum(lf + m0, ig)
    w_i = jnp.exp(ig - m1)
    w_f = jnp.exp(lf + m0 - m1)
    a = w_i * jnp.sum(q * k, axis=-1, keepdims=True)
    cq_rows = []
    for h in range(H_MLSTM):
        qh = jnp.broadcast_to(q[h:h + 1, :], (8, HEAD_DIM)).astype(BF16)
        cq = lax.dot_general(qh, c0_ref[0, h].astype(BF16), _NT, preferred_element_type=F32)
        cq_rows.append(cq[0:1, :])
    cq = jnp.concatenate(cq_rows, axis=0)
    num = a * v + w_f * cq
    den = a + w_f * jnp.sum(n0 * q, axis=-1, keepdims=True)
    hh = num / jnp.maximum(jnp.abs(den), jnp.exp(-m1))
    hm_ref[0] = (_rms(hh, go_ref[...]) * og_ref[0]).astype(BF16)
    n1_ref[0] = w_f * n0 + w_i * k
    m1_ref[0] = m1
    wv = w_i * v
    wv_cols = jnp.transpose(jnp.concatenate([wv, jnp.zeros((HEAD_DIM - H_MLSTM, HEAD_DIM), F32)], axis=0))
    for h in range(H_MLSTM):
        c1_ref[0, h] = w_f[h:h + 1, 0:1] * c0_ref[0, h] + wv_cols[:, h:h + 1] * k[h:h + 1, :]


def _mlstm_step(q, k, v, og, ig, lf, m0, n0, c0, g_out):
    nb = q.shape[0]
    hd = (nb, H_MLSTM, HEAD_DIM)
    tok = pl.BlockSpec((1, H_MLSTM, HEAD_DIM), lambda b: (b, 0, 0))
    mat = pl.BlockSpec((1, H_MLSTM, HEAD_DIM, HEAD_DIM), lambda b: (b, 0, 0, 0))
    return pl.pallas_call(
        _mlstm_step_kernel,
        grid=(nb,),
        in_specs=[tok] * 8 + [mat, pl.BlockSpec((H_MLSTM, HEAD_DIM), lambda b: (0, 0))],
        out_specs=[tok, mat, tok, tok],
        out_shape=[jax.ShapeDtypeStruct(hd, BF16), jax.ShapeDtypeStruct((nb, H_MLSTM, HEAD_DIM, HEAD_DIM), F32),
                   jax.ShapeDtypeStruct(hd, F32), jax.ShapeDtypeStruct(hd, F32)],
        compiler_params=_params(("parallel",), 32),
        name="mlstm_step",
    )(q.reshape(hd), k.reshape(hd), v.reshape(hd), og.reshape(hd), ig, lf, m0, n0, c0, g_out)


def _mem_decode_kernel(q_ref, k_ref, v_ref, o_ref):
    n_mem = k_ref.shape[2] // H_MEM
    for t in range(H_MEM):
        sl = slice(t * HEAD_DIM, (t + 1) * HEAD_DIM)
        rows = pl.ds(t, n_mem, stride=H_MEM)
        q = jnp.broadcast_to(q_ref[0, :, sl], (8, HEAD_DIM))
        s = lax.dot_general(q, k_ref[0, 0, rows, :].astype(BF16), _NT, preferred_element_type=F32) * (HEAD_DIM ** -0.5)
        o = _softmax_pv(s, v_ref[0, 0, rows, :].astype(BF16))
        o_ref[0, :, sl] = o[0:1, :].astype(BF16)


def _memory_attend_decode(q, mem_k, mem_v, layer):
    _, nb, rows, _ = mem_k.shape
    tok = pl.BlockSpec((1, 1, D_MEMH), lambda b: (b, 0, 0))
    mem = pl.BlockSpec((1, 1, rows, HEAD_DIM), lambda b: (layer, b, 0, 0))
    out = pl.pallas_call(
        _mem_decode_kernel,
        grid=(nb,),
        in_specs=[tok, mem, mem],
        out_specs=tok,
        out_shape=jax.ShapeDtypeStruct((nb, 1, D_MEMH), BF16),
        compiler_params=_params(("parallel",), 32),
        name="memory_attend_decode",
    )(q.reshape(nb, 1, D_MEMH), mem_k, mem_v)
    return out.reshape(nb, D_MEMH)


def _outproj_kernel(x_ref, d_ref, m_ref, c_ref, wd_ref, wm_ref, wc_ref, y_ref):
    y = x_ref[...] + _mm(d_ref[...], wd_ref[...])
    y += _mm(m_ref[...], wm_ref[...])
    y += _mm(c_ref[...], wc_ref[...])
    y_ref[...] = y


def _out_project(x, d, m, c, w_d, w_m, w_c):
    rows, dm = x.shape
    tm = _row_tile(rows, 512)

    def row(width):
        return pl.BlockSpec((tm, width), lambda i: (i, 0))

    return pl.pallas_call(
        _outproj_kernel,
        grid=(rows // tm,),
        in_specs=[row(dm), row(D_DIFF), row(D_MLSTM), row(D_MEMH), _resident(w_d.shape), _resident(w_m.shape),
                  _resident(w_c.shape)],
        out_specs=row(dm),
        out_shape=jax.ShapeDtypeStruct((rows, dm), F32),
        compiler_params=_params(("parallel",), 48),
        name="out_project",
    )(x, d, m, c, w_d, w_m, w_c)


def _rope_tables(pos):
    inv = 1.0 / (ROPE_THETA ** (jnp.arange(0, DH_HALF, 2, dtype=F32) / DH_HALF))
    ang = pos.astype(F32)[:, None] * inv[None, :]
    c, s = jnp.cos(ang), jnp.sin(ang)
    return jnp.tile(c, (1, 4)), jnp.concatenate([-s, s, -s, s], axis=1)


def _split_w_in(w_in, dtype):
    o = 0
    wq, wk, wv = (w_in[:, o + t * D_DIFF:o + (t + 1) * D_DIFF].astype(dtype) for t in range(3))
    o = 3 * D_DIFF
    w_ml = tuple(w_in[:, o + t * D_MLSTM:o + (t + 1) * D_MLSTM].astype(dtype) for t in range(4))
    o += 4 * D_MLSTM
    w_gate = jnp.pad(w_in[:, o:o + 2 * H_MLSTM], ((0, 0), (0, GATE_LANES - 2 * H_MLSTM))).astype(dtype)
    o += 2 * H_MLSTM
    w_mq = w_in[:, o:o + D_MEMH].astype(dtype)
    return wq, wk, wv, w_ml, w_gate, w_mq


def _split_w_out(w_out, dtype):
    w = w_out.astype(dtype)
    return w[:D_DIFF], w[D_DIFF:D_DIFF + D_MLSTM], w[D_DIFF + D_MLSTM:]


def kernel(x_prompt, x_sample, cache_diff_k, cache_diff_v, cache_mem_k, cache_mem_v, state_mlstm_C, state_mlstm_n,
           state_mlstm_m, page_table, mem_prompt, g_ffn1, w_ffn1_gate, w_ffn1_up, w_ffn1_down, g_mix, w_in, b_igate,
           b_fgate, g_q_diff, g_k_diff, lambda_q1, lambda_k1, lambda_q2, lambda_k2, g_subln, g_mlstm_out, g_mem_in,
           w_mem_k, w_mem_v, g_q_mem, g_k_mem, w_out, g_ffn2, w_ffn2_gate, w_ffn2_up, w_ffn2_down):
    depth = w_in.shape[0]
    bp, s_len, d_model = x_prompt.shape
    nb, t_s, _ = x_sample.shape
    assert bp == 1 and t_s == 1, "one prompt sequence and one new token per decode request"
    n_phys, page = cache_diff_k.shape[1], cache_diff_k.shape[2]
    n_past = page_table.shape[1] * page
    n_mem = mem_prompt.shape[1]

    xp = x_prompt.reshape(s_len, d_model)
    xs = x_sample.reshape(nb, d_model)
    cos_p, sin_p = _rope_tables(jnp.arange(s_len))
    cos_s, sin_s = (jnp.broadcast_to(t, (nb, LANES)) for t in _rope_tables(jnp.full((1,), n_past)))

    outs = [[] for _ in range(12)]
    for l in range(depth):
        lambda_init = 0.8 - 0.6 * math.exp(-0.3 * l)
        out_scale = 1.0 - lambda_init
        bf = lambda a: a[l].astype(BF16)
        ffn1 = (g_ffn1[l], bf(w_ffn1_gate), bf(w_ffn1_up), bf(w_ffn1_down))
        ffn2 = (g_ffn2[l], bf(w_ffn2_gate), bf(w_ffn2_up), bf(w_ffn2_down))
        wq, wk, wv, w_ml, w_gate, w_mq = _split_w_in(w_in[l], BF16)
        gate_bias = jnp.pad(jnp.concatenate([b_igate[l], b_fgate[l]]).astype(F32),
                            (0, GATE_LANES - 2 * H_MLSTM)).reshape(1, GATE_LANES)
        lam = _diff_lambda(lambda_q1[l], lambda_k1[l], lambda_q2[l], lambda_k2[l], lambda_init)
        g_mo = g_mlstm_out[l].reshape(H_MLSTM, HEAD_DIM)

        xp, hp = _ffn_half(xp, *ffn1, g_next=g_mix[l])
        dq, dk, dv, dk_b, dv_t = _diff_project(hp, wq, wk, wv, g_q_diff[l], g_k_diff[l], cos_p, sin_p,
                                               DH_HALF ** -0.5 * math.log2(math.e), wv_t=wv.T)
        lq, lk, lv, og, gates_t = _mlstm_project(hp, w_ml, w_gate, gate_bias, transpose_gates=True)
        mk, mv, mk_b, mv_b = _memory_kv(mem_prompt.reshape(n_mem, d_model), g_mem_in[l], bf(w_mem_k), bf(w_mem_v),
                                        g_k_mem[l])
        mem_o = _mem_query(hp, w_mq, g_q_mem[l], mk_b, mv_b)
        diff_o = _diff_attention_prompt(lam, dq, dk_b, dv_t, g_subln[l], out_scale)
        h_m, c_p, n_p, m_p = _mlstm_prompt(lq, lk, lv, og, gates_t, g_mo)
        xp = _out_project(xp, diff_o, h_m, mem_o, *_split_w_out(w_out[l], BF16))
        xp = _ffn_half(xp, *ffn2)
        outs[0].append(dk.reshape(1, s_len, H_DIFF, 2, DH_HALF))
        outs[1].append(dv.reshape(1, s_len, H_DIFF, HEAD_DIM))
        outs[2].append(mk.reshape(1, n_mem, H_MEM, HEAD_DIM))
        outs[3].append(mv.reshape(1, n_mem, H_MEM, HEAD_DIM))
        outs[4].append(c_p.reshape(1, H_MLSTM, HEAD_DIM, HEAD_DIM))
        outs[5].append(n_p.reshape(1, H_MLSTM, HEAD_DIM))
        outs[6].append(m_p[:, 0].reshape(1, H_MLSTM))

        ffn1 = (g_ffn1[l], w_ffn1_gate[l], w_ffn1_up[l], w_ffn1_down[l])
        ffn2 = (g_ffn2[l], w_ffn2_gate[l], w_ffn2_up[l], w_ffn2_down[l])
        wq, wk, wv, w_ml, w_gate, w_mq = _split_w_in(w_in[l], F32)
        xs, hs = _ffn_half(xs, *ffn1, g_next=g_mix[l])
        dq, dk, dv = _diff_project(hs, wq, wk, wv, g_q_diff[l], g_k_diff[l], cos_s, sin_s, DH_HALF ** -0.5)
        lq, lk, lv, og, gates = _mlstm_project(hs, w_ml, w_gate, gate_bias, transpose_gates=False)
        mq = _mem_query(hs, w_mq, g_q_mem[l])
        diff_o = _diff_attention_decode(lam, dq, dk, dv, cache_diff_k.reshape(depth, n_phys, page * 2 * H_DIFF, DH_HALF),
                                        cache_diff_v.reshape(depth, n_phys, page * H_DIFF, HEAD_DIM), l, page_table,
                                        g_subln[l], out_scale)
        rep = lambda a: jnp.broadcast_to(a.astype(F32)[:, :, None], (nb, H_MLSTM, HEAD_DIM))
        h_m, c_s, n_s, m_s = _mlstm_step(lq, lk, lv, og, rep(gates[:, :H_MLSTM]), rep(gates[:, H_MLSTM:2 * H_MLSTM]),
                                         rep(state_mlstm_m[l]), state_mlstm_n[l].astype(F32),
                                         state_mlstm_C[l].astype(F32), g_mo)
        mem_o = _memory_attend_decode(mq, cache_mem_k.reshape(depth, nb, n_mem * H_MEM, HEAD_DIM),
                                      cache_mem_v.reshape(depth, nb, n_mem * H_MEM, HEAD_DIM), l)
        xs = _out_project(xs, diff_o, h_m.reshape(nb, D_MLSTM), mem_o, *_split_w_out(w_out[l], F32))
        xs = _ffn_half(xs, *ffn2)
        outs[7].append(dk.reshape(nb, 1, H_DIFF, 2, DH_HALF))
        outs[8].append(dv.reshape(nb, 1, H_DIFF, HEAD_DIM))
        outs[9].append(c_s)
        outs[10].append(n_s)
        outs[11].append(m_s[:, :, 0])

    stacked = [jnp.stack(o, 0) for o in outs]
    return (xp.reshape(1, s_len, d_model), xs.reshape(nb, 1, d_model), *stacked)
```

```python
import functools
import math

import jax
import jax.numpy as jnp
from jax import lax
from jax.experimental import pallas as pl
from jax.experimental.pallas import tpu as pltpu

F32 = jnp.float32
BF16 = jnp.bfloat16

HEAD_DIM = 128
H_DIFF = 8
DH_HALF = HEAD_DIM // 2
H_MLSTM = 4
H_MEM = 4
D_DIFF = H_DIFF * HEAD_DIM
D_MLSTM = H_MLSTM * HEAD_DIM
D_MEMH = H_MEM * HEAD_DIM
ROPE_THETA = 10000.0
EPS = 1e-6
NEG = -1e30
MLSTM_CHUNK = 128

LANES = 128
GATE_LANES = LANES
MIB = 1024 * 1024

_NT = (((1,), (1,)), ((), ()))


def _params(semantics, vmem_mib):
    return pltpu.CompilerParams(dimension_semantics=semantics, vmem_limit_bytes=vmem_mib * MIB)


def _row_tile(m, pref):
    return pref if m % pref == 0 else m


def _rms(x, g):
    return x * lax.rsqrt(jnp.mean(x * x, axis=-1, keepdims=True) + EPS) * g


def _split(x):
    hi = x.astype(BF16)
    return hi, (x - hi.astype(F32)).astype(BF16)


def _mm(x, w):
    if w.dtype == BF16:
        return jnp.dot(x.astype(BF16), w, preferred_element_type=F32)
    wh, wl = _split(w)
    if x.dtype == BF16:
        return jnp.dot(x, wl, preferred_element_type=F32) + jnp.dot(x, wh, preferred_element_type=F32)
    xh, xl = _split(x)
    m = x.shape[0]
    top = jnp.dot(jnp.concatenate([xh, xl], axis=0), wh, preferred_element_type=F32)
    return (top[m:] + jnp.dot(xh, wl, preferred_element_type=F32)) + top[:m]


def _resident(shape):
    return pl.BlockSpec(shape, lambda *_: (0,) * len(shape), pipeline_mode=pl.Buffered(1))


def _log_sigmoid(x):
    return jnp.minimum(x, 0.0) - jnp.log1p(jnp.exp(-jnp.abs(x)))


def _ffn_kernel(*refs, n_ff, next_norm):
    if next_norm:
        x_ref, g_ref, wg_ref, wu_ref, wd_ref, gn_ref, y_ref, hn_ref, h_scr = refs
    else:
        x_ref, g_ref, wg_ref, wu_ref, wd_ref, y_ref, h_scr = refs
    j = pl.program_id(1)

    @pl.when(j == 0)
    def _():
        h_scr[...] = _rms(x_ref[...], g_ref[...]).astype(h_scr.dtype)
        y_ref[...] = jnp.zeros(y_ref.shape, F32)

    h = h_scr[...]
    gate = _mm(h, wg_ref[...])
    up = _mm(h, wu_ref[...])
    y_ref[...] += _mm(gate * jax.nn.sigmoid(gate) * up, wd_ref[...])

    @pl.when(j == n_ff - 1)
    def _():
        y = x_ref[...] + 0.5 * y_ref[...]
        y_ref[...] = y
        if next_norm:
            hn_ref[...] = _rms(y, gn_ref[...]).astype(hn_ref.dtype)


def _ffn_half(x, g, wg, wu, wd, g_next=None):
    m, d = x.shape
    d_ff = wg.shape[1]
    tm = _row_tile(m, 512)
    tf = _row_tile(d_ff, 512)
    n_ff = d_ff // tf
    next_norm = g_next is not None
    row = pl.BlockSpec((tm, d), lambda i, j: (i, 0))
    vec = pl.BlockSpec((1, d), lambda i, j: (0, 0))
    in_specs = [row, vec, pl.BlockSpec((d, tf), lambda i, j: (0, j)), pl.BlockSpec((d, tf), lambda i, j: (0, j)),
                pl.BlockSpec((tf, d), lambda i, j: (j, 0))]
    args = [x, g.reshape(1, d), wg, wu, wd]
    out_shape = [jax.ShapeDtypeStruct((m, d), F32)]
    out_specs = [row]
    if next_norm:
        in_specs.append(vec)
        args.append(g_next.reshape(1, d))
        out_shape.append(jax.ShapeDtypeStruct((m, d), wg.dtype))
        out_specs.append(row)
    out = pl.pallas_call(
        functools.partial(_ffn_kernel, n_ff=n_ff, next_norm=next_norm),
        grid=(m // tm, n_ff),
        in_specs=in_specs,
        out_specs=out_specs,
        out_shape=out_shape,
        scratch_shapes=[pltpu.VMEM((tm, d), wg.dtype)],
        compiler_params=_params(("parallel", "arbitrary"), 48),
        name="ffn_half",
    )(*args)
    return out if next_norm else out[0]


def _diffproj_kernel(*refs, q_scale, prompt):
    if prompt:
        h_ref, wq_ref, wk_ref, wv_ref, gq_ref, gk_ref, cos_ref, sin_ref, seg_ref, wvt_ref, qb_ref, kf_ref, vf_ref, kb_ref, vt_ref = refs
    else:
        h_ref, wq_ref, wk_ref, wv_ref, gq_ref, gk_ref, cos_ref, sin_ref, seg_ref, qb_ref, kf_ref, vf_ref = refs
    h = h_ref[...]
    tm = h.shape[0]
    heads = qb_ref.shape[1] // LANES
    cos = cos_ref[...]
    sin = sin_ref[...]
    seg = seg_ref[...]
    lane = lax.broadcasted_iota(jnp.int32, (tm, LANES), 1)
    first_half = (lane & (DH_HALF - 1)) < (DH_HALF // 2)

    def norm_rope(z, g):
        ss = _mm(z * z, seg)
        y = z * lax.rsqrt(ss * (1.0 / DH_HALF) + EPS) * g
        partner = jnp.where(first_half, pltpu.roll(y, LANES - DH_HALF // 2, axis=1), pltpu.roll(y, DH_HALF // 2, axis=1))
        return y * cos + partner * sin

    zq = _mm(h, wq_ref[...])
    gq = gq_ref[...]
    for t in range(heads):
        sl = slice(t * LANES, (t + 1) * LANES)
        qb_ref[:, sl] = (norm_rope(zq[:, sl], gq) * q_scale).astype(BF16)
    zk = _mm(h, wk_ref[...])
    gk = gk_ref[...]
    for t in range(heads):
        sl = slice(t * LANES, (t + 1) * LANES)
        kt = norm_rope(zk[:, sl], gk)
        kf_ref[:, sl] = kt
        if prompt:
            kb_ref[:, sl] = kt.astype(BF16)
    vf_ref[...] = _mm(h, wv_ref[...])
    if prompt:
        vt_ref[0] = lax.dot_general(wvt_ref[...], h, _NT, preferred_element_type=F32).astype(BF16)


def _diff_project(h, wq, wk, wv, g_q, g_k, cos, sin, q_scale, wv_t=None):
    m, d = h.shape
    tm = _row_tile(m, 512)
    prompt = wv_t is not None
    tn = D_DIFF if wq.dtype == BF16 else 2 * LANES
    seg = jnp.kron(jnp.eye(2, dtype=F32), jnp.ones((DH_HALF, DH_HALF), F32)).astype(wq.dtype)
    row_in = pl.BlockSpec((tm, d), lambda i, j: (i, 0))
    w_spec = _resident((d, tn)) if tn == D_DIFF else pl.BlockSpec((d, tn), lambda i, j: (0, j))
    vec = pl.BlockSpec((1, LANES), lambda i, j: (0, 0))
    tab = pl.BlockSpec((tm, LANES), lambda i, j: (i, 0))
    row_out = pl.BlockSpec((tm, tn), lambda i, j: (i, j))
    in_specs = [row_in, w_spec, w_spec, w_spec, vec, vec, tab, tab, pl.BlockSpec((LANES, LANES), lambda i, j: (0, 0))]
    args = [h, wq, wk, wv, jnp.tile(g_q, 2).reshape(1, LANES), jnp.tile(g_k, 2).reshape(1, LANES), cos, sin, seg]
    out_specs = [row_out] * 3
    out_shape = [jax.ShapeDtypeStruct((m, D_DIFF), dt) for dt in (BF16, F32, F32)]
    if prompt:
        assert tn == D_DIFF
        in_specs.append(_resident((D_DIFF, d)))
        args.append(wv_t)
        out_specs += [row_out, pl.BlockSpec((1, D_DIFF, tm), lambda i, j: (i, 0, 0))]
        out_shape += [jax.ShapeDtypeStruct((m, D_DIFF), BF16), jax.ShapeDtypeStruct((m // tm, D_DIFF, tm), BF16)]
    return pl.pallas_call(
        functools.partial(_diffproj_kernel, q_scale=q_scale, prompt=prompt),
        grid=(m // tm, D_DIFF // tn),
        in_specs=in_specs,
        out_specs=out_specs,
        out_shape=out_shape,
        compiler_params=_params(("parallel", "arbitrary"), 56),
        name="diff_project",
    )(*args)


def _mlstmproj_kernel(h_ref, wq_ref, wk_ref, wv_ref, wo_ref, wg_ref, b_ref, q_ref, k_ref, v_ref, og_ref, gt_ref, *,
                      transpose_gates):
    h = h_ref[...]
    tm = h.shape[0]
    q_ref[...] = _mm(h, wq_ref[...]).astype(q_ref.dtype)
    k_ref[...] = (_mm(h, wk_ref[...]) * (HEAD_DIM ** -0.5)).astype(k_ref.dtype)
    v_ref[...] = _mm(h, wv_ref[...]).astype(v_ref.dtype)
    og_ref[...] = jax.nn.sigmoid(_mm(h, wo_ref[...]))
    zg = _mm(h, wg_ref[...]) + b_ref[...]
    lane = lax.broadcasted_iota(jnp.int32, (tm, GATE_LANES), 1)
    gates = jnp.where(lane < H_MLSTM, zg, _log_sigmoid(zg))
    if transpose_gates:
        for c in range(tm // MLSTM_CHUNK):
            gt = jnp.transpose(gates[c * MLSTM_CHUNK:(c + 1) * MLSTM_CHUNK, :])
            gt_ref[c] = gt[:2 * H_MLSTM, :]
    else:
        gt_ref[...] = gates


def _mlstm_project(h, w, w_gate, bias, transpose_gates):
    m, d = h.shape
    dt = w[0].dtype
    tm = _row_tile(m, 512)
    row_in = pl.BlockSpec((tm, d), lambda i: (i, 0))
    row_out = pl.BlockSpec((tm, D_MLSTM), lambda i: (i, 0))
    if transpose_gates:
        cpt = tm // MLSTM_CHUNK
        g_shape = jax.ShapeDtypeStruct((m // MLSTM_CHUNK, 2 * H_MLSTM, MLSTM_CHUNK), F32)
        g_spec = pl.BlockSpec((cpt, 2 * H_MLSTM, MLSTM_CHUNK), lambda i: (i, 0, 0))
    else:
        g_shape = jax.ShapeDtypeStruct((m, GATE_LANES), F32)
        g_spec = pl.BlockSpec((tm, GATE_LANES), lambda i: (i, 0))
    return pl.pallas_call(
        functools.partial(_mlstmproj_kernel, transpose_gates=transpose_gates),
        grid=(m // tm,),
        in_specs=[row_in] + [_resident((d, D_MLSTM))] * 4 + [_resident((d, GATE_LANES)), _resident((1, GATE_LANES))],
        out_specs=[row_out, row_out, row_out, row_out, g_spec],
        out_shape=[jax.ShapeDtypeStruct((m, D_MLSTM), t) for t in (dt, dt, dt, F32)] + [g_shape],
        compiler_params=_params(("parallel",), 48),
        name="mlstm_project",
    )(h, *w, w_gate, bias)


def _softmax_pv(s, v):
    m = jnp.max(s, axis=-1, keepdims=True)
    p = jnp.exp(s - m)
    l = jnp.sum(p, axis=-1, keepdims=True)
    return jnp.dot(p.astype(BF16), v, preferred_element_type=F32) / l


def _memq_kernel(*refs, attend):
    if attend:
        h_ref, w_ref, gq_ref, mk_ref, mv_ref, o_ref = refs
    else:
        h_ref, w_ref, gq_ref, o_ref = refs
    z = _mm(h_ref[...], w_ref[...])
    gq = gq_ref[...]
    for t in range(H_MEM):
        sl = slice(t * HEAD_DIM, (t + 1) * HEAD_DIM)
        q = _rms(z[:, sl], gq)
        if attend:
            s = lax.dot_general(q.astype(BF16), mk_ref[:, sl], _NT, preferred_element_type=F32) * (HEAD_DIM ** -0.5)
            q = _softmax_pv(s, mv_ref[:, sl])
        o_ref[:, sl] = q.astype(BF16)


def _mem_query(h, w, g_q, mem_k=None, mem_v=None):
    m, d = h.shape
    tm = _row_tile(m, 512)
    attend = mem_k is not None
    in_specs = [pl.BlockSpec((tm, d), lambda i: (i, 0)), pl.BlockSpec((d, D_MEMH), lambda i: (0, 0)),
                pl.BlockSpec((1, HEAD_DIM), lambda i: (0, 0))]
    args = [h, w, g_q.reshape(1, HEAD_DIM)]
    if attend:
        n_mem = mem_k.shape[0]
        in_specs += [pl.BlockSpec((n_mem, D_MEMH), lambda i: (0, 0))] * 2
        args += [mem_k, mem_v]
    return pl.pallas_call(
        functools.partial(_memq_kernel, attend=attend),
        grid=(m // tm,),
        in_specs=in_specs,
        out_specs=pl.BlockSpec((tm, D_MEMH), lambda i: (i, 0)),
        out_shape=jax.ShapeDtypeStruct((m, D_MEMH), BF16),
        compiler_params=_params(("parallel",), 32),
        name="mem_query",
    )(*args)


def _memkv_kernel(mem_ref, g_ref, wk_ref, wv_ref, gk_ref, kf_ref, vf_ref, kb_ref, vb_ref):
    mn = _rms(mem_ref[...], g_ref[...]).astype(BF16)
    zk = jnp.dot(mn, wk_ref[...], preferred_element_type=F32)
    gk = gk_ref[...]
    for t in range(H_MEM):
        sl = slice(t * HEAD_DIM, (t + 1) * HEAD_DIM)
        kt = _rms(zk[:, sl], gk)
        kf_ref[:, sl] = kt
        kb_ref[:, sl] = kt.astype(BF16)
    zv = jnp.dot(mn, wv_ref[...], preferred_element_type=F32)
    vf_ref[...] = zv
    vb_ref[...] = zv.astype(BF16)


def _memory_kv(mem, g_in, wk, wv, g_k):
    n_mem, d = mem.shape
    shapes = [jax.ShapeDtypeStruct((n_mem, D_MEMH), dt) for dt in (F32, F32, BF16, BF16)]
    return pl.pallas_call(
        _memkv_kernel,
        out_shape=shapes,
        compiler_params=pltpu.CompilerParams(vmem_limit_bytes=32 * MIB),
        name="memory_kv",
    )(mem, g_in.reshape(1, d), wk, wv, g_k.reshape(1, HEAD_DIM))


def _lambda_kernel(q1_ref, k1_ref, q2_ref, k2_ref, o_ref, *, lambda_init):
    a = jnp.exp(jnp.sum(q1_ref[...] * k1_ref[...], axis=-1, keepdims=True))
    b = jnp.exp(jnp.sum(q2_ref[...] * k2_ref[...], axis=-1, keepdims=True))
    o_ref[...] = a - b + lambda_init


def _diff_lambda(q1, k1, q2, k2, lambda_init):
    r = lambda a: a.reshape(1, DH_HALF).astype(F32)
    return pl.pallas_call(
        functools.partial(_lambda_kernel, lambda_init=lambda_init),
        out_shape=jax.ShapeDtypeStruct((1, 1), F32),
        name="diff_lambda",
    )(r(q1), r(k1), r(q2), r(k2))


def _split_maps(q):
    lane = lax.broadcasted_iota(jnp.int32, q.shape, 1)
    qf = q.astype(F32)
    return jnp.concatenate([jnp.where(lane < DH_HALF, qf, 0.0), jnp.where(lane >= DH_HALF, qf, 0.0)], axis=0).astype(BF16)


def _subln(d, g, out_scale):
    return _rms(d, g) * out_scale


def _attn_kernel(lam_ref, q_ref, k_ref, vt_ref, gs_ref, o_ref, m_scr, l_scr, acc_scr, *, tq, out_scale):
    i = pl.program_id(1)
    qs = _split_maps(q_ref[...])
    m_scr[...] = jnp.full(m_scr.shape, NEG, F32)
    l_scr[...] = jnp.zeros(l_scr.shape, F32)
    acc_scr[...] = jnp.zeros(acc_scr.shape, F32)

    def step(j, masked):
        kj = k_ref[pl.ds(pl.multiple_of(j * tq, tq), tq), :]
        s = lax.dot_general(kj, qs, _NT, preferred_element_type=F32)
        if masked:
            key = lax.broadcasted_iota(jnp.int32, s.shape, 0)
            qry = lax.broadcasted_iota(jnp.int32, s.shape, 1)
            s = jnp.where(key <= jnp.where(qry >= tq, qry - tq, qry), s, NEG)
        m_old = m_scr[...]
        m_new = jnp.maximum(m_old, jnp.max(s, axis=0, keepdims=True))
        alpha = jnp.exp2(m_old - m_new)
        p = jnp.exp2(s - m_new)
        l_scr[...] = alpha * l_scr[...] + jnp.sum(p, axis=0, keepdims=True)
        pb = p.astype(BF16)
        tv = vt_ref.shape[2]
        pv = jnp.dot(vt_ref[j * (tq // tv)], pb[:tv], preferred_element_type=F32)
        for u in range(1, tq // tv):
            pv += jnp.dot(vt_ref[j * (tq // tv) + u], pb[u * tv:(u + 1) * tv], preferred_element_type=F32)
        acc_scr[...] = alpha * acc_scr[...] + pv
        m_scr[...] = m_new

    def body(j, carry):
        step(j, False)
        return carry

    lax.fori_loop(0, i, body, 0)
    step(i, True)

    o_t = acc_scr[...] / l_scr[...]
    d = jnp.transpose(o_t[:, :tq] - lam_ref[0, 0] * o_t[:, tq:])
    o_ref[...] = _subln(d, gs_ref[...], out_scale).astype(BF16)


def _diff_attention_prompt(lam, q, k, v_t, g_subln, out_scale):
    s_len = q.shape[0]
    tv = v_t.shape[2]
    tq = _row_tile(s_len, 1024)
    assert tq % tv == 0
    blk = pl.BlockSpec((tq, HEAD_DIM), lambda h, i: (i, h))
    return pl.pallas_call(
        functools.partial(_attn_kernel, tq=tq, out_scale=out_scale),
        grid=(H_DIFF, s_len // tq),
        in_specs=[pl.BlockSpec(memory_space=pltpu.SMEM), blk, pl.BlockSpec((s_len, HEAD_DIM), lambda h, i: (0, h)),
                  pl.BlockSpec((s_len // tv, HEAD_DIM, tv), lambda h, i: (0, h, 0)),
                  pl.BlockSpec((1, HEAD_DIM), lambda h, i: (0, 0))],
        out_specs=blk,
        out_shape=jax.ShapeDtypeStruct((s_len, D_DIFF), BF16),
        scratch_shapes=[pltpu.VMEM((1, 2 * tq), F32), pltpu.VMEM((1, 2 * tq), F32), pltpu.VMEM((HEAD_DIM, 2 * tq), F32)],
        compiler_params=_params(("parallel", "arbitrary"), 48),
        name="diff_attention_prompt",
    )(lam, q, k, v_t, g_subln.reshape(1, HEAD_DIM))


def _decode_attn_kernel(pt_ref, lam_ref, q_ref, kn_ref, vn_ref, gs_ref, *refs, pages, out_scale):
    k_refs = refs[:pages]
    v_refs = refs[pages:2 * pages]
    o_ref, qs_scr, m_scr, l_scr, acc_scr = refs[2 * pages:]
    p_idx = pl.program_id(1)
    n_maps = 2 * H_DIFF
    page = k_refs[0].shape[1]

    @pl.when(p_idx == 0)
    def _():
        row = lax.broadcasted_iota(jnp.int32, (n_maps, D_DIFF), 0)
        col = lax.broadcasted_iota(jnp.int32, (n_maps, D_DIFF), 1)
        qb = jnp.broadcast_to(q_ref[0].astype(F32), (n_maps, D_DIFF))
        qs_scr[...] = jnp.where(col // DH_HALF == row, qb, 0.0).astype(BF16)
        m_scr[...] = jnp.full(m_scr.shape, NEG, F32)
        l_scr[...] = jnp.zeros(l_scr.shape, F32)
        acc_scr[...] = jnp.zeros(acc_scr.shape, F32)

    qs = qs_scr[...]
    s = jnp.concatenate(
        [lax.dot_general(qs, k_refs[t][0], _NT, preferred_element_type=F32) for t in range(pages)], axis=1)
    m_old = m_scr[...]
    m_new = jnp.maximum(m_old, jnp.max(s, axis=-1, keepdims=True))
    alpha = jnp.exp(m_old - m_new)
    p = jnp.exp(s - m_new).astype(BF16)
    l_scr[...] = alpha * l_scr[...] + jnp.sum(p.astype(F32), axis=-1, keepdims=True)
    for h in range(H_DIFF):
        sl = slice(h * HEAD_DIM, (h + 1) * HEAD_DIM)
        rows = pl.ds(h, page, stride=H_DIFF)
        pv = jnp.dot(p[:, :page], v_refs[0][0, 0, rows, :].astype(BF16), preferred_element_type=F32)
        for t in range(1, pages):
            pv += jnp.dot(p[:, t * page:(t + 1) * page], v_refs[t][0, 0, rows, :].astype(BF16),
                          preferred_element_type=F32)
        acc_scr[:, sl] = alpha * acc_scr[:, sl] + pv
    m_scr[...] = m_new

    @pl.when(p_idx == pl.num_programs(1) - 1)
    def _():
        s_new = jnp.sum(qs.astype(F32) * kn_ref[0], axis=-1, keepdims=True)
        m_fin = jnp.maximum(m_new, s_new)
        a_fin = jnp.exp(m_new - m_fin)
        p_new = jnp.exp(s_new - m_fin)
        l_fin = a_fin * l_scr[...] + p_new
        o = (a_fin * acc_scr[...] + p_new * vn_ref[0]) / l_fin
        lam = lam_ref[0, 0]
        gs = gs_ref[...]
        for h in range(H_DIFF):
            sl = slice(h * HEAD_DIM, (h + 1) * HEAD_DIM)
            d = o[2 * h:2 * h + 1, sl] - lam * o[2 * h + 1:2 * h + 2, sl]
            o_ref[0, :, sl] = _subln(d, gs, out_scale).astype(BF16)


def _diff_attention_decode(lam, q, k_new, v_new, cache_k, cache_v, layer, page_table, g_subln, out_scale,
                           pages_per_step=8):
    nb = q.shape[0]
    n_pages = page_table.shape[1]
    n_maps = 2 * H_DIFF
    page = cache_k.shape[1]
    pages = pages_per_step if n_pages % pages_per_step == 0 else 1
    tok = pl.BlockSpec((1, 1, D_DIFF), lambda b, p, pt: (b, 0, 0))

    def k_spec(t):
        return pl.BlockSpec((1, page, D_DIFF), lambda b, p, pt: (pt[b, p * pages + t], 0, 0))

    def v_spec(t):
        return pl.BlockSpec((1, 1, page * H_DIFF, HEAD_DIM), lambda b, p, pt: (layer, pt[b, p * pages + t], 0, 0))

    grid_spec = pltpu.PrefetchScalarGridSpec(
        num_scalar_prefetch=1,
        grid=(nb, n_pages // pages),
        in_specs=[pl.BlockSpec(memory_space=pltpu.SMEM), tok, tok, tok, pl.BlockSpec((1, HEAD_DIM), lambda b, p, pt: (0, 0))]
        + [k_spec(t) for t in range(pages)] + [v_spec(t) for t in range(pages)],
        out_specs=tok,
        scratch_shapes=[pltpu.VMEM((n_maps, D_DIFF), BF16), pltpu.VMEM((n_maps, 1), F32), pltpu.VMEM((n_maps, 1), F32),
                        pltpu.VMEM((n_maps, D_DIFF), F32)],
    )
    out = pl.pallas_call(
        functools.partial(_decode_attn_kernel, pages=pages, out_scale=out_scale),
        grid_spec=grid_spec,
        out_shape=jax.ShapeDtypeStruct((nb, 1, D_DIFF), BF16),
        compiler_params=_params(("parallel", "arbitrary"), 48),
        name="diff_attention_decode",
    )(page_table, lam, q.reshape(nb, 1, D_DIFF), k_new.reshape(nb, 1, D_DIFF), v_new.reshape(nb, 1, D_DIFF),
      g_subln.reshape(1, HEAD_DIM), *([cache_k] * pages), *([cache_v] * pages))
    return out.reshape(nb, D_DIFF)


def _mlstm_kernel(q_ref, k_ref, v_ref, og_ref, gt_ref, go_ref, hm_ref, c_ref, n_ref, m_ref):
    @pl.when(pl.program_id(0) == 0)
    def _():
        c_ref[...] = jnp.zeros(c_ref.shape, F32)
        n_ref[...] = jnp.zeros(n_ref.shape, F32)
        m_ref[...] = jnp.zeros(m_ref.shape, F32)

    L = MLSTM_CHUNK
    lane8 = lax.broadcasted_iota(jnp.int32, (2 * H_MLSTM, L), 1)
    row = lax.broadcasted_iota(jnp.int32, (L, L), 0)
    col = lax.broadcasted_iota(jnp.int32, (L, L), 1)
    causal = col <= row

    def chunk(c, carry):
        r0 = pl.multiple_of(c * L, L)
        g8 = gt_ref[c]
        cs = g8
        d = 1
        while d < L:
            cs = cs + jnp.where(lane8 >= d, pltpu.roll(cs, d, axis=1), 0.0)
            d *= 2
        stacked = jnp.concatenate([g8, cs, jnp.zeros((L - 4 * H_MLSTM, L), F32)], axis=0)
        cols = jnp.transpose(stacked)
        for h in range(H_MLSTM):
            sl = slice(h * HEAD_DIM, (h + 1) * HEAD_DIM)
            q = q_ref[pl.ds(r0, L), sl]
            k = k_ref[pl.ds(r0, L), sl]
            v = v_ref[pl.ds(r0, L), sl]
            ig_r = g8[h:h + 1, :]
            bt_r = cs[H_MLSTM + h:H_MLSTM + h + 1, :]
            ig_c = cols[:, h:h + 1]
            bt_c = cols[:, 3 * H_MLSTM + h:3 * H_MLSTM + h + 1]
            m0 = m_ref[h:h + 1, 0:1]
            c0 = c_ref[h]
            n0 = n_ref[h:h + 1, :]

            dm = jnp.where(causal, bt_c + (ig_r - bt_r), NEG)
            inter = bt_c + m0
            m_c = jnp.maximum(inter, jnp.max(dm, axis=-1, keepdims=True))
            w_intra = jnp.exp(dm - m_c)
            w_inter = jnp.exp(inter - m_c)
            a = w_intra * lax.dot_general(q, k, _NT, preferred_element_type=F32)
            cq = lax.dot_general(q, c0.astype(BF16), _NT, preferred_element_type=F32)
            num = jnp.dot(a.astype(BF16), v, preferred_element_type=F32) + w_inter * cq
            nq = jnp.sum(q.astype(F32) * n0, axis=-1, keepdims=True)
            den = jnp.sum(a, axis=-1, keepdims=True) + w_inter * nq
            hh = num / jnp.maximum(jnp.abs(den), jnp.exp(-m_c))
            hm_ref[pl.ds(r0, L), sl] = (_rms(hh, go_ref[:, sl]) * og_ref[pl.ds(r0, L), sl]).astype(BF16)

            bl = bt_r[:, L - 1:L]
            ml = m_c[L - 1:L, :]
            w_end = jnp.exp(bl - bt_c + ig_c - ml)
            decay = jnp.exp(bl + m0 - ml)
            vw_t = jnp.transpose(w_end * v.astype(F32)).astype(BF16)
            c_ref[h] = decay * c0 + jnp.dot(vw_t, k, preferred_element_type=F32)
            n_ref[h:h + 1, :] = decay * n0 + jnp.sum(w_end * k.astype(F32), axis=0, keepdims=True)
            m_ref[h:h + 1, :] = jnp.broadcast_to(ml, (1, LANES))
        return carry

    lax.fori_loop(0, gt_ref.shape[0], chunk, 0)


def _mlstm_prompt(q, k, v, og, gates_t, g_out):
    s_len = q.shape[0]
    tm = _row_tile(s_len, 512)
    cpt = tm // MLSTM_CHUNK
    row = pl.BlockSpec((tm, D_MLSTM), lambda t: (t, 0))
    return pl.pallas_call(
        _mlstm_kernel,
        grid=(s_len // tm,),
        in_specs=[row, row, row, row, pl.BlockSpec((cpt, 2 * H_MLSTM, MLSTM_CHUNK), lambda t: (t, 0, 0)),
                  pl.BlockSpec((1, D_MLSTM), lambda t: (0, 0))],
        out_specs=[row, pl.BlockSpec((H_MLSTM, HEAD_DIM, HEAD_DIM), lambda t: (0, 0, 0)),
                   pl.BlockSpec((H_MLSTM, HEAD_DIM), lambda t: (0, 0)), pl.BlockSpec((H_MLSTM, LANES), lambda t: (0, 0))],
        out_shape=[jax.ShapeDtypeStruct((s_len, D_MLSTM), BF16), jax.ShapeDtypeStruct((H_MLSTM, HEAD_DIM, HEAD_DIM), F32),
                   jax.ShapeDtypeStruct((H_MLSTM, HEAD_DIM), F32), jax.ShapeDtypeStruct((H_MLSTM, LANES), F32)],
        compiler_params=_params(("arbitrary",), 32),
        name="mlstm_prompt",
    )(q, k, v, og, gates_t, g_out.reshape(1, D_MLSTM))


def _mlstm_step_kernel(q_ref, k_ref, v_ref, og_ref, ig_ref, lf_ref, m0_ref, n0_ref, c0_ref, go_ref,
                       hm_ref, c1_ref, n1_ref, m1_ref):
    q = q_ref[0].astype(F32)
    k = k_ref[0].astype(F32)
    v = v_ref[0].astype(F32)
    ig = ig_ref[0]
    lf = lf_ref[0]
    m0 = m0_ref[0]
    n0 = n0_ref[0]
    m1 = jnp.maximum(lf + m0, ig)
    w_i = jnp.exp(ig - m1)
    w_f = jnp.exp(lf + m0 - m1)
    a = w_i * jnp.sum(q * k, axis=-1, keepdims=True)
    cq_rows = []
    for h in range(H_MLSTM):
        qh = jnp.broadcast_to(q[h:h + 1, :], (8, HEAD_DIM)).astype(BF16)
        cq = lax.dot_general(qh, c0_ref[0, h].astype(BF16), _NT, preferred_element_type=F32)
        cq_rows.append(cq[0:1, :])
    cq = jnp.concatenate(cq_rows, axis=0)
    num = a * v + w_f * cq
    den = a + w_f * jnp.sum(n0 * q, axis=-1, keepdims=True)
    hh = num / jnp.maximum(jnp.abs(den), jnp.exp(-m1))
    hm_ref[0] = (_rms(hh, go_ref[...]) * og_ref[0]).astype(BF16)
    n1_ref[0] = w_f * n0 + w_i * k
    m1_ref[0] = m1
    wv = w_i * v
    wv_cols = jnp.transpose(jnp.concatenate([wv, jnp.zeros((HEAD_DIM - H_MLSTM, HEAD_DIM), F32)], axis=0))
    for h in range(H_MLSTM):
        c1_ref[0, h] = w_f[h:h + 1, 0:1] * c0_ref[0, h] + wv_cols[:, h:h + 1] * k[h:h + 1, :]


def _mlstm_step(q, k, v, og, ig, lf, m0, n0, c0, g_out):
    nb = q.shape[0]
    hd = (nb, H_MLSTM, HEAD_DIM)
    tok = pl.BlockSpec((1, H_MLSTM, HEAD_DIM), lambda b: (b, 0, 0))
    mat = pl.BlockSpec((1, H_MLSTM, HEAD_DIM, HEAD_DIM), lambda b: (b, 0, 0, 0))
    return pl.pallas_call(
        _mlstm_step_kernel,
        grid=(nb,),
        in_specs=[tok] * 8 + [mat, pl.BlockSpec((H_MLSTM, HEAD_DIM), lambda b: (0, 0))],
        out_specs=[tok, mat, tok, tok],
        out_shape=[jax.ShapeDtypeStruct(hd, BF16), jax.ShapeDtypeStruct((nb, H_MLSTM, HEAD_DIM, HEAD_DIM), F32),
                   jax.ShapeDtypeStruct(hd, F32), jax.ShapeDtypeStruct(hd, F32)],
        compiler_params=_params(("parallel",), 32),
        name="mlstm_step",
    )(q.reshape(hd), k.reshape(hd), v.reshape(hd), og.reshape(hd), ig, lf, m0, n0, c0, g_out)


def _mem_decode_kernel(q_ref, k_ref, v_ref, o_ref):
    n_mem = k_ref.shape[2] // H_MEM
    for t in range(H_MEM):
        sl = slice(t * HEAD_DIM, (t + 1) * HEAD_DIM)
        rows = pl.ds(t, n_mem, stride=H_MEM)
        q = jnp.broadcast_to(q_ref[0, :, sl], (8, HEAD_DIM))
        s = lax.dot_general(q, k_ref[0, 0, rows, :].astype(BF16), _NT, preferred_element_type=F32) * (HEAD_DIM ** -0.5)
        o = _softmax_pv(s, v_ref[0, 0, rows, :].astype(BF16))
        o_ref[0, :, sl] = o[0:1, :].astype(BF16)


def _memory_attend_decode(q, mem_k, mem_v, layer):
    _, nb, rows, _ = mem_k.shape
    tok = pl.BlockSpec((1, 1, D_MEMH), lambda b: (b, 0, 0))
    mem = pl.BlockSpec((1, 1, rows, HEAD_DIM), lambda b: (layer, b, 0, 0))
    out = pl.pallas_call(
        _mem_decode_kernel,
        grid=(nb,),
        in_specs=[tok, mem, mem],
        out_specs=tok,
        out_shape=jax.ShapeDtypeStruct((nb, 1, D_MEMH), BF16),
        compiler_params=_params(("parallel",), 32),
        name="memory_attend_decode",
    )(q.reshape(nb, 1, D_MEMH), mem_k, mem_v)
    return out.reshape(nb, D_MEMH)


def _outproj_kernel(x_ref, d_ref, m_ref, c_ref, wd_ref, wm_ref, wc_ref, y_ref):
    y = x_ref[...] + _mm(d_ref[...], wd_ref[...])
    y += _mm(m_ref[...], wm_ref[...])
    y += _mm(c_ref[...], wc_ref[...])
    y_ref[...] = y


def _out_project(x, d, m, c, w_d, w_m, w_c):
    rows, dm = x.shape
    tm = _row_tile(rows, 512)

    def row(width):
        return pl.BlockSpec((tm, width), lambda i: (i, 0))

    return pl.pallas_call(
        _outproj_kernel,
        grid=(rows // tm,),
        in_specs=[row(dm), row(D_DIFF), row(D_MLSTM), row(D_MEMH), _resident(w_d.shape), _resident(w_m.shape),
                  _resident(w_c.shape)],
        out_specs=row(dm),
        out_shape=jax.ShapeDtypeStruct((rows, dm), F32),
        compiler_params=_params(("parallel",), 48),
        name="out_project",
    )(x, d, m, c, w_d, w_m, w_c)


def _rope_tables(pos):
    inv = 1.0 / (ROPE_THETA ** (jnp.arange(0, DH_HALF, 2, dtype=F32) / DH_HALF))
    ang = pos.astype(F32)[:, None] * inv[None, :]
    c, s = jnp.cos(ang), jnp.sin(ang)
    return jnp.tile(c, (1, 4)), jnp.concatenate([-s, s, -s, s], axis=1)


def _split_w_in(w_in, dtype):
    o = 0
    wq, wk, wv = (w_in[:, o + t * D_DIFF:o + (t + 1) * D_DIFF].astype(dtype) for t in range(3))
    o = 3 * D_DIFF
    w_ml = tuple(w_in[:, o + t * D_MLSTM:o + (t + 1) * D_MLSTM].astype(dtype) for t in range(4))
    o += 4 * D_MLSTM
    w_gate = jnp.pad(w_in[:, o:o + 2 * H_MLSTM], ((0, 0), (0, GATE_LANES - 2 * H_MLSTM))).astype(dtype)
    o += 2 * H_MLSTM
    w_mq = w_in[:, o:o + D_MEMH].astype(dtype)
    return wq, wk, wv, w_ml, w_gate, w_mq


def _split_w_out(w_out, dtype):
    w = w_out.astype(dtype)
    return w[:D_DIFF], w[D_DIFF:D_DIFF + D_MLSTM], w[D_DIFF + D_MLSTM:]


def kernel(x_prompt, x_sample, cache_diff_k, cache_diff_v, cache_mem_k, cache_mem_v, state_mlstm_C, state_mlstm_n,
           state_mlstm_m, page_table, mem_prompt, g_ffn1, w_ffn1_gate, w_ffn1_up, w_ffn1_down, g_mix, w_in, b_igate,
           b_fgate, g_q_diff, g_k_diff, lambda_q1, lambda_k1, lambda_q2, lambda_k2, g_subln, g_mlstm_out, g_mem_in,
           w_mem_k, w_mem_v, g_q_mem, g_k_mem, w_out, g_ffn2, w_ffn2_gate, w_ffn2_up, w_ffn2_down):
    depth = w_in.shape[0]
    bp, s_len, d_model = x_prompt.shape
    nb, t_s, _ = x_sample.shape
    assert bp == 1 and t_s == 1, "one prompt sequence and one new token per decode request"
    n_phys, page = cache_diff_k.shape[1], cache_diff_k.shape[2]
    n_past = page_table.shape[1] * page
    n_mem = mem_prompt.shape[1]

    xp = x_prompt.reshape(s_len, d_model)
    xs = x_sample.reshape(nb, d_model)
    cos_p, sin_p = _rope_tables(jnp.arange(s_len))
    cos_s, sin_s = (jnp.broadcast_to(t, (nb, LANES)) for t in _rope_tables(jnp.full((1,), n_past)))

    outs = [[] for _ in range(12)]
    for l in range(depth):
        lambda_init = 0.8 - 0.6 * math.exp(-0.3 * l)
        out_scale = 1.0 - lambda_init
        bf = lambda a: a[l].astype(BF16)
        ffn1 = (g_ffn1[l], bf(w_ffn1_gate), bf(w_ffn1_up), bf(w_ffn1_down))
        ffn2 = (g_ffn2[l], bf(w_ffn2_gate), bf(w_ffn2_up), bf(w_ffn2_down))
        wq, wk, wv, w_ml, w_gate, w_mq = _split_w_in(w_in[l], BF16)
        gate_bias = jnp.pad(jnp.concatenate([b_igate[l], b_fgate[l]]).astype(F32),
                            (0, GATE_LANES - 2 * H_MLSTM)).reshape(1, GATE_LANES)
        lam = _diff_lambda(lambda_q1[l], lambda_k1[l], lambda_q2[l], lambda_k2[l], lambda_init)
        g_mo = g_mlstm_out[l].reshape(H_MLSTM, HEAD_DIM)

        xp, hp = _ffn_half(xp, *ffn1, g_next=g_mix[l])
        dq, dk, dv, dk_b, dv_t = _diff_project(hp, wq, wk, wv, g_q_diff[l], g_k_diff[l], cos_p, sin_p,
                                               DH_HALF ** -0.5 * math.log2(math.e), wv_t=wv.T)
        lq, lk, lv, og, gates_t = _mlstm_project(hp, w_ml, w_gate, gate_bias, transpose_gates=True)
        mk, mv, mk_b, mv_b = _memory_kv(mem_prompt.reshape(n_mem, d_model), g_mem_in[l], bf(w_mem_k), bf(w_mem_v),
                                        g_k_mem[l])
        mem_o = _mem_query(hp, w_mq, g_q_mem[l], mk_b, mv_b)
        diff_o = _diff_attention_prompt(lam, dq, dk_b, dv_t, g_subln[l], out_scale)
        h_m, c_p, n_p, m_p = _mlstm_prompt(lq, lk, lv, og, gates_t, g_mo)
        xp = _out_project(xp, diff_o, h_m, mem_o, *_split_w_out(w_out[l], BF16))
        xp = _ffn_half(xp, *ffn2)
        outs[0].append(dk.reshape(1, s_len, H_DIFF, 2, DH_HALF))
        outs[1].append(dv.reshape(1, s_len, H_DIFF, HEAD_DIM))
        outs[2].append(mk.reshape(1, n_mem, H_MEM, HEAD_DIM))
        outs[3].append(mv.reshape(1, n_mem, H_MEM, HEAD_DIM))
        outs[4].append(c_p.reshape(1, H_MLSTM, HEAD_DIM, HEAD_DIM))
        outs[5].append(n_p.reshape(1, H_MLSTM, HEAD_DIM))
        outs[6].append(m_p[:, 0].reshape(1, H_MLSTM))

        ffn1 = (g_ffn1[l], w_ffn1_gate[l], w_ffn1_up[l], w_ffn1_down[l])
        ffn2 = (g_ffn2[l], w_ffn2_gate[l], w_ffn2_up[l], w_ffn2_down[l])
        wq, wk, wv, w_ml, w_gate, w_mq = _split_w_in(w_in[l], F32)
        xs, hs = _ffn_half(xs, *ffn1, g_next=g_mix[l])
        dq, dk, dv = _diff_project(hs, wq, wk, wv, g_q_diff[l], g_k_diff[l], cos_s, sin_s, DH_HALF ** -0.5)
        lq, lk, lv, og, gates = _mlstm_project(hs, w_ml, w_gate, gate_bias, transpose_gates=False)
        mq = _mem_query(hs, w_mq, g_q_mem[l])
        diff_o = _diff_attention_decode(lam, dq, dk, dv, cache_diff_k[l].astype(BF16).reshape(n_phys, page, D_DIFF),
                                        cache_diff_v.reshape(depth, n_phys, page * H_DIFF, HEAD_DIM), l, page_table,
                                        g_subln[l], out_scale)
        rep = lambda a: jnp.broadcast_to(a.astype(F32)[:, :, None], (nb, H_MLSTM, HEAD_DIM))
        h_m, c_s, n_s, m_s = _mlstm_step(lq, lk, lv, og, rep(gates[:, :H_MLSTM]), rep(gates[:, H_MLSTM:2 * H_MLSTM]),
                                         rep(state_mlstm_m[l]), state_mlstm_n[l].astype(F32),
                                         state_mlstm_C[l].astype(F32), g_mo)
        mem_o = _memory_attend_decode(mq, cache_mem_k.reshape(depth, nb, n_mem * H_MEM, HEAD_DIM),
                                      cache_mem_v.reshape(depth, nb, n_mem * H_MEM, HEAD_DIM), l)
        xs = _out_project(xs, diff_o, h_m.reshape(nb, D_MLSTM), mem_o, *_split_w_out(w_out[l], F32))
        xs = _ffn_half(xs, *ffn2)
        outs[7].append(dk.reshape(nb, 1, H_DIFF, 2, DH_HALF))
        outs[8].append(dv.reshape(nb, 1, H_DIFF, HEAD_DIM))
        outs[9].append(c_s)
        outs[10].append(n_s)
        outs[11].append(m_s[:, :, 0])

    stacked = [jnp.stack(o, 0) for o in outs]
    return (xp.reshape(1, s_len, d_model), xs.reshape(nb, 1, d_model), *stacked)
```

```python
import functools
import math

import jax
import jax.numpy as jnp
from jax import lax
from jax.experimental import pallas as pl
from jax.experimental.pallas import tpu as pltpu

F32 = jnp.float32
BF16 = jnp.bfloat16

HEAD_DIM = 128
H_DIFF = 8
DH_HALF = HEAD_DIM // 2
H_MLSTM = 4
H_MEM = 4
D_DIFF = H_DIFF * HEAD_DIM
D_MLSTM = H_MLSTM * HEAD_DIM
D_MEMH = H_MEM * HEAD_DIM
ROPE_THETA = 10000.0
EPS = 1e-6
NEG = -1e30
MLSTM_CHUNK = 128

LANES = 128
GATE_LANES = LANES
MIB = 1024 * 1024

_NT = (((1,), (1,)), ((), ()))


def _params(semantics, vmem_mib):
    return pltpu.CompilerParams(dimension_semantics=semantics, vmem_limit_bytes=vmem_mib * MIB)


def _row_tile(m, pref):
    return pref if m % pref == 0 else m


def _rms(x, g):
    return x * lax.rsqrt(jnp.mean(x * x, axis=-1, keepdims=True) + EPS) * g


def _split(x):
    hi = x.astype(BF16)
    return hi, (x - hi.astype(F32)).astype(BF16)


def _mm(x, w):
    if w.dtype == BF16:
        return jnp.dot(x.astype(BF16), w, preferred_element_type=F32)
    wh, wl = _split(w)
    if x.dtype == BF16:
        return jnp.dot(x, wl, preferred_element_type=F32) + jnp.dot(x, wh, preferred_element_type=F32)
    xh, xl = _split(x)
    m = x.shape[0]
    top = jnp.dot(jnp.concatenate([xh, xl], axis=0), wh, preferred_element_type=F32)
    return (top[m:] + jnp.dot(xh, wl, preferred_element_type=F32)) + top[:m]


def _resident(shape):
    return pl.BlockSpec(shape, lambda *_: (0,) * len(shape), pipeline_mode=pl.Buffered(1))


def _log_sigmoid(x):
    return jnp.minimum(x, 0.0) - jnp.log1p(jnp.exp(-jnp.abs(x)))


def _ffn_kernel(*refs, n_ff, next_norm):
    if next_norm:
        x_ref, g_ref, wg_ref, wu_ref, wd_ref, gn_ref, y_ref, hn_ref, h_scr = refs
    else:
        x_ref, g_ref, wg_ref, wu_ref, wd_ref, y_ref, h_scr = refs
    j = pl.program_id(1)

    @pl.when(j == 0)
    def _():
        h_scr[...] = _rms(x_ref[...], g_ref[...]).astype(h_scr.dtype)
        y_ref[...] = jnp.zeros(y_ref.shape, F32)

    h = h_scr[...]
    gate = _mm(h, wg_ref[...])
    up = _mm(h, wu_ref[...])
    y_ref[...] += _mm(gate * jax.nn.sigmoid(gate) * up, wd_ref[...])

    @pl.when(j == n_ff - 1)
    def _():
        y = x_ref[...] + 0.5 * y_ref[...]
        y_ref[...] = y
        if next_norm:
            hn_ref[...] = _rms(y, gn_ref[...]).astype(hn_ref.dtype)


def _ffn_half(x, g, wg, wu, wd, g_next=None):
    m, d = x.shape
    d_ff = wg.shape[1]
    tm = _row_tile(m, 512)
    tf = _row_tile(d_ff, 512)
    n_ff = d_ff // tf
    next_norm = g_next is not None
    row = pl.BlockSpec((tm, d), lambda i, j: (i, 0))
    vec = pl.BlockSpec((1, d), lambda i, j: (0, 0))
    in_specs = [row, vec, pl.BlockSpec((d, tf), lambda i, j: (0, j)), pl.BlockSpec((d, tf), lambda i, j: (0, j)),
                pl.BlockSpec((tf, d), lambda i, j: (j, 0))]
    args = [x, g.reshape(1, d), wg, wu, wd]
    out_shape = [jax.ShapeDtypeStruct((m, d), F32)]
    out_specs = [row]
    if next_norm:
        in_specs.append(vec)
        args.append(g_next.reshape(1, d))
        out_shape.append(jax.ShapeDtypeStruct((m, d), wg.dtype))
        out_specs.append(row)
    out = pl.pallas_call(
        functools.partial(_ffn_kernel, n_ff=n_ff, next_norm=next_norm),
        grid=(m // tm, n_ff),
        in_specs=in_specs,
        out_specs=out_specs,
        out_shape=out_shape,
        scratch_shapes=[pltpu.VMEM((tm, d), wg.dtype)],
        compiler_params=_params(("parallel", "arbitrary"), 48),
        name="ffn_half",
    )(*args)
    return out if next_norm else out[0]


def _diffproj_kernel(*refs, q_scale, prompt):
    if prompt:
        h_ref, wq_ref, wk_ref, wv_ref, gq_ref, gk_ref, cos_ref, sin_ref, seg_ref, wvt_ref, qb_ref, kf_ref, vf_ref, kb_ref, vt_ref = refs
    else:
        h_ref, wq_ref, wk_ref, wv_ref, gq_ref, gk_ref, cos_ref, sin_ref, seg_ref, qb_ref, kf_ref, vf_ref = refs
    h = h_ref[...]
    tm = h.shape[0]
    heads = qb_ref.shape[1] // LANES
    cos = cos_ref[...]
    sin = sin_ref[...]
    seg = seg_ref[...]
    lane = lax.broadcasted_iota(jnp.int32, (tm, LANES), 1)
    first_half = (lane & (DH_HALF - 1)) < (DH_HALF // 2)

    def norm_rope(z, g):
        ss = _mm(z * z, seg)
        y = z * lax.rsqrt(ss * (1.0 / DH_HALF) + EPS) * g
        partner = jnp.where(first_half, pltpu.roll(y, LANES - DH_HALF // 2, axis=1), pltpu.roll(y, DH_HALF // 2, axis=1))
        return y * cos + partner * sin

    zq = _mm(h, wq_ref[...])
    gq = gq_ref[...]
    for t in range(heads):
        sl = slice(t * LANES, (t + 1) * LANES)
        qb_ref[:, sl] = (norm_rope(zq[:, sl], gq) * q_scale).astype(BF16)
    zk = _mm(h, wk_ref[...])
    gk = gk_ref[...]
    for t in range(heads):
        sl = slice(t * LANES, (t + 1) * LANES)
        kt = norm_rope(zk[:, sl], gk)
        kf_ref[:, sl] = kt
        if prompt:
            kb_ref[:, sl] = kt.astype(BF16)
    vf_ref[...] = _mm(h, wv_ref[...])
    if prompt:
        vt_ref[0] = lax.dot_general(wvt_ref[...], h, _NT, preferred_element_type=F32).astype(BF16)


def _diff_project(h, wq, wk, wv, g_q, g_k, cos, sin, q_scale, wv_t=None):
    m, d = h.shape
    tm = _row_tile(m, 512)
    prompt = wv_t is not None
    tn = D_DIFF if wq.dtype == BF16 else 2 * LANES
    seg = jnp.kron(jnp.eye(2, dtype=F32), jnp.ones((DH_HALF, DH_HALF), F32)).astype(wq.dtype)
    row_in = pl.BlockSpec((tm, d), lambda i, j: (i, 0))
    w_spec = _resident((d, tn)) if tn == D_DIFF else pl.BlockSpec((d, tn), lambda i, j: (0, j))
    vec = pl.BlockSpec((1, LANES), lambda i, j: (0, 0))
    tab = pl.BlockSpec((tm, LANES), lambda i, j: (i, 0))
    row_out = pl.BlockSpec((tm, tn), lambda i, j: (i, j))
    in_specs = [row_in, w_spec, w_spec, w_spec, vec, vec, tab, tab, pl.BlockSpec((LANES, LANES), lambda i, j: (0, 0))]
    args = [h, wq, wk, wv, jnp.tile(g_q, 2).reshape(1, LANES), jnp.tile(g_k, 2).reshape(1, LANES), cos, sin, seg]
    out_specs = [row_out] * 3
    out_shape = [jax.ShapeDtypeStruct((m, D_DIFF), dt) for dt in (BF16, F32, F32)]
    if prompt:
        assert tn == D_DIFF
        in_specs.append(_resident((D_DIFF, d)))
        args.append(wv_t)
        out_specs += [row_out, pl.BlockSpec((1, D_DIFF, tm), lambda i, j: (i, 0, 0))]
        out_shape += [jax.ShapeDtypeStruct((m, D_DIFF), BF16), jax.ShapeDtypeStruct((m // tm, D_DIFF, tm), BF16)]
    return pl.pallas_call(
        functools.partial(_diffproj_kernel, q_scale=q_scale, prompt=prompt),
        grid=(m // tm, D_DIFF // tn),
        in_specs=in_specs,
        out_specs=out_specs,
        out_shape=out_shape,
        compiler_params=_params(("parallel", "arbitrary"), 56),
        name="diff_project",
    )(*args)


def _mlstmproj_kernel(h_ref, wq_ref, wk_ref, wv_ref, wo_ref, wg_ref, b_ref, q_ref, k_ref, v_ref, og_ref, gt_ref, *,
                      transpose_gates):
    h = h_ref[...]
    tm = h.shape[0]
    q_ref[...] = _mm(h, wq_ref[...]).astype(q_ref.dtype)
    k_ref[...] = (_mm(h, wk_ref[...]) * (HEAD_DIM ** -0.5)).astype(k_ref.dtype)
    v_ref[...] = _mm(h, wv_ref[...]).astype(v_ref.dtype)
    og_ref[...] = jax.nn.sigmoid(_mm(h, wo_ref[...]))
    zg = _mm(h, wg_ref[...]) + b_ref[...]
    lane = lax.broadcasted_iota(jnp.int32, (tm, GATE_LANES), 1)
    gates = jnp.where(lane < H_MLSTM, zg, _log_sigmoid(zg))
    if transpose_gates:
        for c in range(tm // MLSTM_CHUNK):
            gt = jnp.transpose(gates[c * MLSTM_CHUNK:(c + 1) * MLSTM_CHUNK, :])
            gt_ref[c] = gt[:2 * H_MLSTM, :]
    else:
        gt_ref[...] = gates


def _mlstm_project(h, w, w_gate, bias, transpose_gates):
    m, d = h.shape
    dt = w[0].dtype
    tm = _row_tile(m, 512)
    row_in = pl.BlockSpec((tm, d), lambda i: (i, 0))
    row_out = pl.BlockSpec((tm, D_MLSTM), lambda i: (i, 0))
    if transpose_gates:
        cpt = tm // MLSTM_CHUNK
        g_shape = jax.ShapeDtypeStruct((m // MLSTM_CHUNK, 2 * H_MLSTM, MLSTM_CHUNK), F32)
        g_spec = pl.BlockSpec((cpt, 2 * H_MLSTM, MLSTM_CHUNK), lambda i: (i, 0, 0))
    else:
        g_shape = jax.ShapeDtypeStruct((m, GATE_LANES), F32)
        g_spec = pl.BlockSpec((tm, GATE_LANES), lambda i: (i, 0))
    return pl.pallas_call(
        functools.partial(_mlstmproj_kernel, transpose_gates=transpose_gates),
        grid=(m // tm,),
        in_specs=[row_in] + [_resident((d, D_MLSTM))] * 4 + [_resident((d, GATE_LANES)), _resident((1, GATE_LANES))],
        out_specs=[row_out, row_out, row_out, row_out, g_spec],
        out_shape=[jax.ShapeDtypeStruct((m, D_MLSTM), t) for t in (dt, dt, dt, F32)] + [g_shape],
        compiler_params=_params(("parallel",), 48),
        name="mlstm_project",
    )(h, *w, w_gate, bias)


def _softmax_pv(s, v):
    m = jnp.max(s, axis=-1, keepdims=True)
    p = jnp.exp(s - m)
    l = jnp.sum(p, axis=-1, keepdims=True)
    return jnp.dot(p.astype(BF16), v, preferred_element_type=F32) / l


def _memq_kernel(*refs, attend):
    if attend:
        h_ref, w_ref, gq_ref, mk_ref, mv_ref, o_ref = refs
    else:
        h_ref, w_ref, gq_ref, o_ref = refs
    z = _mm(h_ref[...], w_ref[...])
    gq = gq_ref[...]
    for t in range(H_MEM):
        sl = slice(t * HEAD_DIM, (t + 1) * HEAD_DIM)
        q = _rms(z[:, sl], gq)
        if attend:
            s = lax.dot_general(q.astype(BF16), mk_ref[:, sl], _NT, preferred_element_type=F32) * (HEAD_DIM ** -0.5)
            q = _softmax_pv(s, mv_ref[:, sl])
        o_ref[:, sl] = q.astype(BF16)


def _mem_query(h, w, g_q, mem_k=None, mem_v=None):
    m, d = h.shape
    tm = _row_tile(m, 512)
    attend = mem_k is not None
    in_specs = [pl.BlockSpec((tm, d), lambda i: (i, 0)), pl.BlockSpec((d, D_MEMH), lambda i: (0, 0)),
                pl.BlockSpec((1, HEAD_DIM), lambda i: (0, 0))]
    args = [h, w, g_q.reshape(1, HEAD_DIM)]
    if attend:
        n_mem = mem_k.shape[0]
        in_specs += [pl.BlockSpec((n_mem, D_MEMH), lambda i: (0, 0))] * 2
        args += [mem_k, mem_v]
    return pl.pallas_call(
        functools.partial(_memq_kernel, attend=attend),
        grid=(m // tm,),
        in_specs=in_specs,
        out_specs=pl.BlockSpec((tm, D_MEMH), lambda i: (i, 0)),
        out_shape=jax.ShapeDtypeStruct((m, D_MEMH), BF16),
        compiler_params=_params(("parallel",), 32),
        name="mem_query",
    )(*args)


def _memkv_kernel(mem_ref, g_ref, wk_ref, wv_ref, gk_ref, kf_ref, vf_ref, kb_ref, vb_ref):
    mn = _rms(mem_ref[...], g_ref[...]).astype(BF16)
    zk = jnp.dot(mn, wk_ref[...], preferred_element_type=F32)
    gk = gk_ref[...]
    for t in range(H_MEM):
        sl = slice(t * HEAD_DIM, (t + 1) * HEAD_DIM)
        kt = _rms(zk[:, sl], gk)
        kf_ref[:, sl] = kt
        kb_ref[:, sl] = kt.astype(BF16)
    zv = jnp.dot(mn, wv_ref[...], preferred_element_type=F32)
    vf_ref[...] = zv
    vb_ref[...] = zv.astype(BF16)


def _memory_kv(mem, g_in, wk, wv, g_k):
    n_mem, d = mem.shape
    shapes = [jax.ShapeDtypeStruct((n_mem, D_MEMH), dt) for dt in (F32, F32, BF16, BF16)]
    return pl.pallas_call(
        _memkv_kernel,
        out_shape=shapes,
        compiler_params=pltpu.CompilerParams(vmem_limit_bytes=32 * MIB),
        name="memory_kv",
    )(mem, g_in.reshape(1, d), wk, wv, g_k.reshape(1, HEAD_DIM))


def _lambda_kernel(q1_ref, k1_ref, q2_ref, k2_ref, o_ref, *, lambda_init):
    a = jnp.exp(jnp.sum(q1_ref[...] * k1_ref[...], axis=-1, keepdims=True))
    b = jnp.exp(jnp.sum(q2_ref[...] * k2_ref[...], axis=-1, keepdims=True))
    o_ref[...] = a - b + lambda_init


def _diff_lambda(q1, k1, q2, k2, lambda_init):
    r = lambda a: a.reshape(1, DH_HALF).astype(F32)
    return pl.pallas_call(
        functools.partial(_lambda_kernel, lambda_init=lambda_init),
        out_shape=jax.ShapeDtypeStruct((1, 1), F32),
        name="diff_lambda",
    )(r(q1), r(k1), r(q2), r(k2))


def _split_maps(q):
    lane = lax.broadcasted_iota(jnp.int32, q.shape, 1)
    qf = q.astype(F32)
    return jnp.concatenate([jnp.where(lane < DH_HALF, qf, 0.0), jnp.where(lane >= DH_HALF, qf, 0.0)], axis=0).astype(BF16)


def _subln(d, g, out_scale):
    return _rms(d, g) * out_scale


def _attn_kernel(lam_ref, q_ref, k_ref, vt_ref, gs_ref, o_ref, m_scr, l_scr, acc_scr, *, tq, out_scale):
    i = pl.program_id(1)
    qs = _split_maps(q_ref[...])
    m_scr[...] = jnp.full(m_scr.shape, NEG, F32)
    l_scr[...] = jnp.zeros(l_scr.shape, F32)
    acc_scr[...] = jnp.zeros(acc_scr.shape, F32)

    def step(j, masked):
        kj = k_ref[pl.ds(pl.multiple_of(j * tq, tq), tq), :]
        s = lax.dot_general(kj, qs, _NT, preferred_element_type=F32)
        if masked:
            key = lax.broadcasted_iota(jnp.int32, s.shape, 0)
            qry = lax.broadcasted_iota(jnp.int32, s.shape, 1)
            s = jnp.where(key <= jnp.where(qry >= tq, qry - tq, qry), s, NEG)
        m_old = m_scr[...]
        m_new = jnp.maximum(m_old, jnp.max(s, axis=0, keepdims=True))
        alpha = jnp.exp2(m_old - m_new)
        p = jnp.exp2(s - m_new)
        l_scr[...] = alpha * l_scr[...] + jnp.sum(p, axis=0, keepdims=True)
        pb = p.astype(BF16)
        tv = vt_ref.shape[2]
        pv = jnp.dot(vt_ref[j * (tq // tv)], pb[:tv], preferred_element_type=F32)
        for u in range(1, tq // tv):
            pv += jnp.dot(vt_ref[j * (tq // tv) + u], pb[u * tv:(u + 1) * tv], preferred_element_type=F32)
        acc_scr[...] = alpha * acc_scr[...] + pv
        m_scr[...] = m_new

    def body(j, carry):
        step(j, False)
        return carry

    lax.fori_loop(0, i, body, 0)
    step(i, True)

    o_t = acc_scr[...] / l_scr[...]
    d = jnp.transpose(o_t[:, :tq] - lam_ref[0, 0] * o_t[:, tq:])
    o_ref[...] = _subln(d, gs_ref[...], out_scale).astype(BF16)


def _diff_attention_prompt(lam, q, k, v_t, g_subln, out_scale):
    s_len = q.shape[0]
    tv = v_t.shape[2]
    tq = _row_tile(s_len, 1024)
    assert tq % tv == 0
    blk = pl.BlockSpec((tq, HEAD_DIM), lambda h, i: (i, h))
    return pl.pallas_call(
        functools.partial(_attn_kernel, tq=tq, out_scale=out_scale),
        grid=(H_DIFF, s_len // tq),
        in_specs=[pl.BlockSpec(memory_space=pltpu.SMEM), blk, pl.BlockSpec((s_len, HEAD_DIM), lambda h, i: (0, h)),
                  pl.BlockSpec((s_len // tv, HEAD_DIM, tv), lambda h, i: (0, h, 0)),
                  pl.BlockSpec((1, HEAD_DIM), lambda h, i: (0, 0))],
        out_specs=blk,
        out_shape=jax.ShapeDtypeStruct((s_len, D_DIFF), BF16),
        scratch_shapes=[pltpu.VMEM((1, 2 * tq), F32), pltpu.VMEM((1, 2 * tq), F32), pltpu.VMEM((HEAD_DIM, 2 * tq), F32)],
        compiler_params=_params(("parallel", "arbitrary"), 48),
        name="diff_attention_prompt",
    )(lam, q, k, v_t, g_subln.reshape(1, HEAD_DIM))


def _decode_attn_kernel(pt_ref, lam_ref, q_ref, kn_ref, vn_ref, gs_ref, *refs, pages, out_scale):
    k_refs = refs[:pages]
    v_refs = refs[pages:2 * pages]
    o_ref, qs_scr, m_scr, l_scr, acc_scr = refs[2 * pages:]
    p_idx = pl.program_id(1)
    n_maps = 2 * H_DIFF
    page = k_refs[0].shape[2] // H_DIFF

    @pl.when(p_idx == 0)
    def _():
        row = lax.broadcasted_iota(jnp.int32, (n_maps, D_DIFF), 0)
        col = lax.broadcasted_iota(jnp.int32, (n_maps, D_DIFF), 1)
        qb = jnp.broadcast_to(q_ref[0].astype(F32), (n_maps, D_DIFF))
        qs_scr[...] = jnp.where(col // DH_HALF == row, qb, 0.0).astype(BF16)
        m_scr[...] = jnp.full(m_scr.shape, NEG, F32)
        l_scr[...] = jnp.zeros(l_scr.shape, F32)
        acc_scr[...] = jnp.zeros(acc_scr.shape, F32)

    s_pages = []
    for t in range(pages):
        s_t = None
        for h in range(H_DIFF):
            k_h = k_refs[t][0, 0, pl.ds(h, page, stride=H_DIFF), :].astype(BF16)
            s_h = lax.dot_general(qs_scr[:, h * HEAD_DIM:(h + 1) * HEAD_DIM], k_h, _NT, preferred_element_type=F32)
            s_t = s_h if s_t is None else s_t + s_h
        s_pages.append(s_t)
    s = jnp.concatenate(s_pages, axis=1)
    m_old = m_scr[...]
    m_new = jnp.maximum(m_old, jnp.max(s, axis=-1, keepdims=True))
    alpha = jnp.exp(m_old - m_new)
    p = jnp.exp(s - m_new).astype(BF16)
    l_scr[...] = alpha * l_scr[...] + jnp.sum(p.astype(F32), axis=-1, keepdims=True)
    for h in range(H_DIFF):
        sl = slice(h * HEAD_DIM, (h + 1) * HEAD_DIM)
        rows = pl.ds(h, page, stride=H_DIFF)
        pv = jnp.dot(p[:, :page], v_refs[0][0, 0, rows, :].astype(BF16), preferred_element_type=F32)
        for t in range(1, pages):
            pv += jnp.dot(p[:, t * page:(t + 1) * page], v_refs[t][0, 0, rows, :].astype(BF16),
                          preferred_element_type=F32)
        acc_scr[:, sl] = alpha * acc_scr[:, sl] + pv
    m_scr[...] = m_new

    @pl.when(p_idx == pl.num_programs(1) - 1)
    def _():
        s_new = jnp.sum(qs_scr[...].astype(F32) * kn_ref[0], axis=-1, keepdims=True)
        m_fin = jnp.maximum(m_new, s_new)
        a_fin = jnp.exp(m_new - m_fin)
        p_new = jnp.exp(s_new - m_fin)
        l_fin = a_fin * l_scr[...] + p_new
        o = (a_fin * acc_scr[...] + p_new * vn_ref[0]) / l_fin
        lam = lam_ref[0, 0]
        gs = gs_ref[...]
        for h in range(H_DIFF):
            sl = slice(h * HEAD_DIM, (h + 1) * HEAD_DIM)
            d = o[2 * h:2 * h + 1, sl] - lam * o[2 * h + 1:2 * h + 2, sl]
            o_ref[0, :, sl] = _subln(d, gs, out_scale).astype(BF16)


def _diff_attention_decode(lam, q, k_new, v_new, cache_k, cache_v, layer, page_table, g_subln, out_scale,
                           pages_per_step=8):
    nb = q.shape[0]
    n_pages = page_table.shape[1]
    n_maps = 2 * H_DIFF
    rows = cache_k.shape[2]
    pages = pages_per_step if n_pages % pages_per_step == 0 else 1
    tok = pl.BlockSpec((1, 1, D_DIFF), lambda b, p, pt: (b, 0, 0))

    def page_spec(t):
        return pl.BlockSpec((1, 1, rows, HEAD_DIM), lambda b, p, pt: (layer, pt[b, p * pages + t], 0, 0))

    grid_spec = pltpu.PrefetchScalarGridSpec(
        num_scalar_prefetch=1,
        grid=(nb, n_pages // pages),
        in_specs=[pl.BlockSpec(memory_space=pltpu.SMEM), tok, tok, tok, pl.BlockSpec((1, HEAD_DIM), lambda b, p, pt: (0, 0))]
        + [page_spec(t) for t in range(pages)] * 2,
        out_specs=tok,
        scratch_shapes=[pltpu.VMEM((n_maps, D_DIFF), BF16), pltpu.VMEM((n_maps, 1), F32), pltpu.VMEM((n_maps, 1), F32),
                        pltpu.VMEM((n_maps, D_DIFF), F32)],
    )
    out = pl.pallas_call(
        functools.partial(_decode_attn_kernel, pages=pages, out_scale=out_scale),
        grid_spec=grid_spec,
        out_shape=jax.ShapeDtypeStruct((nb, 1, D_DIFF), BF16),
        compiler_params=_params(("parallel", "arbitrary"), 48),
        name="diff_attention_decode",
    )(page_table, lam, q.reshape(nb, 1, D_DIFF), k_new.reshape(nb, 1, D_DIFF), v_new.reshape(nb, 1, D_DIFF),
      g_subln.reshape(1, HEAD_DIM), *([cache_k] * pages), *([cache_v] * pages))
    return out.reshape(nb, D_DIFF)


def _mlstm_kernel(q_ref, k_ref, v_ref, og_ref, gt_ref, go_ref, hm_ref, c_ref, n_ref, m_ref):
    @pl.when(pl.program_id(0) == 0)
    def _():
        c_ref[...] = jnp.zeros(c_ref.shape, F32)
        n_ref[...] = jnp.zeros(n_ref.shape, F32)
        m_ref[...] = jnp.zeros(m_ref.shape, F32)

    L = MLSTM_CHUNK
    lane8 = lax.broadcasted_iota(jnp.int32, (2 * H_MLSTM, L), 1)
    row = lax.broadcasted_iota(jnp.int32, (L, L), 0)
    col = lax.broadcasted_iota(jnp.int32, (L, L), 1)
    causal = col <= row

    def chunk(c, carry):
        r0 = pl.multiple_of(c * L, L)
        g8 = gt_ref[c]
        cs = g8
        d = 1
        while d < L:
            cs = cs + jnp.where(lane8 >= d, pltpu.roll(cs, d, axis=1), 0.0)
            d *= 2
        stacked = jnp.concatenate([g8, cs, jnp.zeros((L - 4 * H_MLSTM, L), F32)], axis=0)
        cols = jnp.transpose(stacked)
        for h in range(H_MLSTM):
            sl = slice(h * HEAD_DIM, (h + 1) * HEAD_DIM)
            q = q_ref[pl.ds(r0, L), sl]
            k = k_ref[pl.ds(r0, L), sl]
            v = v_ref[pl.ds(r0, L), sl]
            ig_r = g8[h:h + 1, :]
            bt_r = cs[H_MLSTM + h:H_MLSTM + h + 1, :]
            ig_c = cols[:, h:h + 1]
            bt_c = cols[:, 3 * H_MLSTM + h:3 * H_MLSTM + h + 1]
            m0 = m_ref[h:h + 1, 0:1]
            c0 = c_ref[h]
            n0 = n_ref[h:h + 1, :]

            dm = jnp.where(causal, bt_c + (ig_r - bt_r), NEG)
            inter = bt_c + m0
            m_c = jnp.maximum(inter, jnp.max(dm, axis=-1, keepdims=True))
            w_intra = jnp.exp(dm - m_c)
            w_inter = jnp.exp(inter - m_c)
            a = w_intra * lax.dot_general(q, k, _NT, preferred_element_type=F32)
            cq = lax.dot_general(q, c0.astype(BF16), _NT, preferred_element_type=F32)
            num = jnp.dot(a.astype(BF16), v, preferred_element_type=F32) + w_inter * cq
            nq = jnp.sum(q.astype(F32) * n0, axis=-1, keepdims=True)
            den = jnp.sum(a, axis=-1, keepdims=True) + w_inter * nq
            hh = num / jnp.maximum(jnp.abs(den), jnp.exp(-m_c))
            hm_ref[pl.ds(r0, L), sl] = (_rms(hh, go_ref[:, sl]) * og_ref[pl.ds(r0, L), sl]).astype(BF16)

            bl = bt_r[:, L - 1:L]
            ml = m_c[L - 1:L, :]
            w_end = jnp.exp(bl - bt_c + ig_c - ml)
            decay = jnp.exp(bl + m0 - ml)
            vw_t = jnp.transpose(w_end * v.astype(F32)).astype(BF16)
            c_ref[h] = decay * c0 + jnp.dot(vw_t, k, preferred_element_type=F32)
            n_ref[h:h + 1, :] = decay * n0 + jnp.sum(w_end * k.astype(F32), axis=0, keepdims=True)
            m_ref[h:h + 1, :] = jnp.broadcast_to(ml, (1, LANES))
        return carry

    lax.fori_loop(0, gt_ref.shape[0], chunk, 0)


def _mlstm_prompt(q, k, v, og, gates_t, g_out):
    s_len = q.shape[0]
    tm = _row_tile(s_len, 512)
    cpt = tm // MLSTM_CHUNK
    row = pl.BlockSpec((tm, D_MLSTM), lambda t: (t, 0))
    return pl.pallas_call(
        _mlstm_kernel,
        grid=(s_len // tm,),
        in_specs=[row, row, row, row, pl.BlockSpec((cpt, 2 * H_MLSTM, MLSTM_CHUNK), lambda t: (t, 0, 0)),
                  pl.BlockSpec((1, D_MLSTM), lambda t: (0, 0))],
        out_specs=[row, pl.BlockSpec((H_MLSTM, HEAD_DIM, HEAD_DIM), lambda t: (0, 0, 0)),
                   pl.BlockSpec((H_MLSTM, HEAD_DIM), lambda t: (0, 0)), pl.BlockSpec((H_MLSTM, LANES), lambda t: (0, 0))],
        out_shape=[jax.ShapeDtypeStruct((s_len, D_MLSTM), BF16), jax.ShapeDtypeStruct((H_MLSTM, HEAD_DIM, HEAD_DIM), F32),
                   jax.ShapeDtypeStruct((H_MLSTM, HEAD_DIM), F32), jax.ShapeDtypeStruct((H_MLSTM, LANES), F32)],
        compiler_params=_params(("arbitrary",), 32),
        name="mlstm_prompt",
    )(q, k, v, og, gates_t, g_out.reshape(1, D_MLSTM))


def _mlstm_step_kernel(q_ref, k_ref, v_ref, og_ref, ig_ref, lf_ref, m0_ref, n0_ref, c0_ref, go_ref,
                       hm_ref, c1_ref, n1_ref, m1_ref):
    q = q_ref[0].astype(F32)
    k = k_ref[0].astype(F32)
    v = v_ref[0].astype(F32)
    ig = ig_ref[0]
    lf = lf_ref[0]
    m0 = m0_ref[0]
    n0 = n0_ref[0]
    m1 = jnp.maximum(lf + m0, ig)
    w_i = jnp.exp(ig - m1)
    w_f = jnp.exp(lf + m0 - m1)
    a = w_i * jnp.sum(q * k, axis=-1, keepdims=True)
    cq_rows = []
    for h in range(H_MLSTM):
        qh = jnp.broadcast_to(q[h:h + 1, :], (8, HEAD_DIM)).astype(BF16)
        cq = lax.dot_general(qh, c0_ref[0, h].astype(BF16), _NT, preferred_element_type=F32)
        cq_rows.append(cq[0:1, :])
    cq = jnp.concatenate(cq_rows, axis=0)
    num = a * v + w_f * cq
    den = a + w_f * jnp.sum(n0 * q, axis=-1, keepdims=True)
    hh = num / jnp.maximum(jnp.abs(den), jnp.exp(-m1))
    hm_ref[0] = (_rms(hh, go_ref[...]) * og_ref[0]).astype(BF16)
    n1_ref[0] = w_f * n0 + w_i * k
    m1_ref[0] = m1
    wv = w_i * v
    wv_cols = jnp.transpose(jnp.concatenate([wv, jnp.zeros((HEAD_DIM - H_MLSTM, HEAD_DIM), F32)], axis=0))
    for h in range(H_MLSTM):
        c1_ref[0, h] = w_f[h:h + 1, 0:1] * c0_ref[0, h] + wv_cols[:, h:h + 1] * k[h:h + 1, :]


def _mlstm_step(q, k, v, og, ig, lf, m0, n0, c0, g_out):
    nb = q.shape[0]
    hd = (nb, H_MLSTM, HEAD_DIM)
    tok = pl.BlockSpec((1, H_MLSTM, HEAD_DIM), lambda b: (b, 0, 0))
    mat = pl.BlockSpec((1, H_MLSTM, HEAD_DIM, HEAD_DIM), lambda b: (b, 0, 0, 0))
    return pl.pallas_call(
        _mlstm_step_kernel,
        grid=(nb,),
        in_specs=[tok] * 8 + [mat, pl.BlockSpec((H_MLSTM, HEAD_DIM), lambda b: (0, 0))],
        out_specs=[tok, mat, tok, tok],
        out_shape=[jax.ShapeDtypeStruct(hd, BF16), jax.ShapeDtypeStruct((nb, H_MLSTM, HEAD_DIM, HEAD_DIM), F32),
                   jax.ShapeDtypeStruct(hd, F32), jax.ShapeDtypeStruct(hd, F32)],
        compiler_params=_params(("parallel",), 32),
        name="mlstm_step",
    )(q.reshape(hd), k.reshape(hd), v.reshape(hd), og.reshape(hd), ig, lf, m0, n0, c0, g_out)


def _mem_decode_kernel(q_ref, k_ref, v_ref, o_ref):
    n_mem = k_ref.shape[2] // H_MEM
    for t in range(H_MEM):
        sl = slice(t * HEAD_DIM, (t + 1) * HEAD_DIM)
        rows = pl.ds(t, n_mem, stride=H_MEM)
        q = jnp.broadcast_to(q_ref[0, :, sl], (8, HEAD_DIM))
        s = lax.dot_general(q, k_ref[0, 0, rows, :].astype(BF16), _NT, preferred_element_type=F32) * (HEAD_DIM ** -0.5)
        o = _softmax_pv(s, v_ref[0, 0, rows, :].astype(BF16))
        o_ref[0, :, sl] = o[0:1, :].astype(BF16)


def _memory_attend_decode(q, mem_k, mem_v, layer):
    _, nb, rows, _ = mem_k.shape
    tok = pl.BlockSpec((1, 1, D_MEMH), lambda b: (b, 0, 0))
    mem = pl.BlockSpec((1, 1, rows, HEAD_DIM), lambda b: (layer, b, 0, 0))
    out = pl.pallas_call(
        _mem_decode_kernel,
        grid=(nb,),
        in_specs=[tok, mem, mem],
        out_specs=tok,
        out_shape=jax.ShapeDtypeStruct((nb, 1, D_MEMH), BF16),
        compiler_params=_params(("parallel",), 32),
        name="memory_attend_decode",
    )(q.reshape(nb, 1, D_MEMH), mem_k, mem_v)
    return out.reshape(nb, D_MEMH)


def _outproj_kernel(x_ref, d_ref, m_ref, c_ref, wd_ref, wm_ref, wc_ref, y_ref):
    y = x_ref[...] + _mm(d_ref[...], wd_ref[...])
    y += _mm(m_ref[...], wm_ref[...])
    y += _mm(c_ref[...], wc_ref[...])
    y_ref[...] = y


def _out_project(x, d, m, c, w_d, w_m, w_c):
    rows, dm = x.shape
    tm = _row_tile(rows, 512)

    def row(width):
        return pl.BlockSpec((tm, width), lambda i: (i, 0))

    return pl.pallas_call(
        _outproj_kernel,
        grid=(rows // tm,),
        in_specs=[row(dm), row(D_DIFF), row(D_MLSTM), row(D_MEMH), _resident(w_d.shape), _resident(w_m.shape),
                  _resident(w_c.shape)],
        out_specs=row(dm),
        out_shape=jax.ShapeDtypeStruct((rows, dm), F32),
        compiler_params=_params(("parallel",), 48),
        name="out_project",
    )(x, d, m, c, w_d, w_m, w_c)


def _rope_tables(pos):
    inv = 1.0 / (ROPE_THETA ** (jnp.arange(0, DH_HALF, 2, dtype=F32) / DH_HALF))
    ang = pos.astype(F32)[:, None] * inv[None, :]
    c, s = jnp.cos(ang), jnp.sin(ang)
    return jnp.tile(c, (1, 4)), jnp.concatenate([-s, s, -s, s], axis=1)


def _split_w_in(w_in, dtype):
    o = 0
    wq, wk, wv = (w_in[:, o + t * D_DIFF:o + (t + 1) * D_DIFF].astype(dtype) for t in range(3))
    o = 3 * D_DIFF
    w_ml = tuple(w_in[:, o + t * D_MLSTM:o + (t + 1) * D_MLSTM].astype(dtype) for t in range(4))
    o += 4 * D_MLSTM
    w_gate = jnp.pad(w_in[:, o:o + 2 * H_MLSTM], ((0, 0), (0, GATE_LANES - 2 * H_MLSTM))).astype(dtype)
    o += 2 * H_MLSTM
    w_mq = w_in[:, o:o + D_MEMH].astype(dtype)
    return wq, wk, wv, w_ml, w_gate, w_mq


def _split_w_out(w_out, dtype):
    w = w_out.astype(dtype)
    return w[:D_DIFF], w[D_DIFF:D_DIFF + D_MLSTM], w[D_DIFF + D_MLSTM:]


def kernel(x_prompt, x_sample, cache_diff_k, cache_diff_v, cache_mem_k, cache_mem_v, state_mlstm_C, state_mlstm_n,
           state_mlstm_m, page_table, mem_prompt, g_ffn1, w_ffn1_gate, w_ffn1_up, w_ffn1_down, g_mix, w_in, b_igate,
           b_fgate, g_q_diff, g_k_diff, lambda_q1, lambda_k1, lambda_q2, lambda_k2, g_subln, g_mlstm_out, g_mem_in,
           w_mem_k, w_mem_v, g_q_mem, g_k_mem, w_out, g_ffn2, w_ffn2_gate, w_ffn2_up, w_ffn2_down):
    depth = w_in.shape[0]
    bp, s_len, d_model = x_prompt.shape
    nb, t_s, _ = x_sample.shape
    assert bp == 1 and t_s == 1, "one prompt sequence and one new token per decode request"
    n_phys, page = cache_diff_k.shape[1], cache_diff_k.shape[2]
    n_past = page_table.shape[1] * page
    n_mem = mem_prompt.shape[1]

    xp = x_prompt.reshape(s_len, d_model)
    xs = x_sample.reshape(nb, d_model)
    cos_p, sin_p = _rope_tables(jnp.arange(s_len))
    cos_s, sin_s = (jnp.broadcast_to(t, (nb, LANES)) for t in _rope_tables(jnp.full((1,), n_past)))

    outs = [[] for _ in range(12)]
    for l in range(depth):
        lambda_init = 0.8 - 0.6 * math.exp(-0.3 * l)
        out_scale = 1.0 - lambda_init
        bf = lambda a: a[l].astype(BF16)
        ffn1 = (g_ffn1[l], bf(w_ffn1_gate), bf(w_ffn1_up), bf(w_ffn1_down))
        ffn2 = (g_ffn2[l], bf(w_ffn2_gate), bf(w_ffn2_up), bf(w_ffn2_down))
        wq, wk, wv, w_ml, w_gate, w_mq = _split_w_in(w_in[l], BF16)
        gate_bias = jnp.pad(jnp.concatenate([b_igate[l], b_fgate[l]]).astype(F32),
                            (0, GATE_LANES - 2 * H_MLSTM)).reshape(1, GATE_LANES)
        lam = _diff_lambda(lambda_q1[l], lambda_k1[l], lambda_q2[l], lambda_k2[l], lambda_init)
        g_mo = g_mlstm_out[l].reshape(H_MLSTM, HEAD_DIM)

        xp, hp = _ffn_half(xp, *ffn1, g_next=g_mix[l])
        dq, dk, dv, dk_b, dv_t = _diff_project(hp, wq, wk, wv, g_q_diff[l], g_k_diff[l], cos_p, sin_p,
                                               DH_HALF ** -0.5 * math.log2(math.e), wv_t=wv.T)
        lq, lk, lv, og, gates_t = _mlstm_project(hp, w_ml, w_gate, gate_bias, transpose_gates=True)
        mk, mv, mk_b, mv_b = _memory_kv(mem_prompt.reshape(n_mem, d_model), g_mem_in[l], bf(w_mem_k), bf(w_mem_v),
                                        g_k_mem[l])
        mem_o = _mem_query(hp, w_mq, g_q_mem[l], mk_b, mv_b)
        diff_o = _diff_attention_prompt(lam, dq, dk_b, dv_t, g_subln[l], out_scale)
        h_m, c_p, n_p, m_p = _mlstm_prompt(lq, lk, lv, og, gates_t, g_mo)
        xp = _out_project(xp, diff_o, h_m, mem_o, *_split_w_out(w_out[l], BF16))
        xp = _ffn_half(xp, *ffn2)
        outs[0].append(dk.reshape(1, s_len, H_DIFF, 2, DH_HALF))
        outs[1].append(dv.reshape(1, s_len, H_DIFF, HEAD_DIM))
        outs[2].append(mk.reshape(1, n_mem, H_MEM, HEAD_DIM))
        outs[3].append(mv.reshape(1, n_mem, H_MEM, HEAD_DIM))
        outs[4].append(c_p.reshape(1, H_MLSTM, HEAD_DIM, HEAD_DIM))
        outs[5].append(n_p.reshape(1, H_MLSTM, HEAD_DIM))
        outs[6].append(m_p[:, 0].reshape(1, H_MLSTM))

        ffn1 = (g_ffn1[l], w_ffn1_gate[l], w_ffn1_up[l], w_ffn1_down[l])
        ffn2 = (g_ffn2[l], w_ffn2_gate[l], w_ffn2_up[l], w_ffn2_down[l])
        wq, wk, wv, w_ml, w_gate, w_mq = _split_w_in(w_in[l], F32)
        xs, hs = _ffn_half(xs, *ffn1, g_next=g_mix[l])
        dq, dk, dv = _diff_project(hs, wq, wk, wv, g_q_diff[l], g_k_diff[l], cos_s, sin_s, DH_HALF ** -0.5)
        lq, lk, lv, og, gates = _mlstm_project(hs, w_ml, w_gate, gate_bias, transpose_gates=False)
        mq = _mem_query(hs, w_mq, g_q_mem[l])
        diff_o = _diff_attention_decode(lam, dq, dk, dv, cache_diff_k.reshape(depth, n_phys, page * H_DIFF, HEAD_DIM),
                                        cache_diff_v.reshape(depth, n_phys, page * H_DIFF, HEAD_DIM), l, page_table,
                                        g_subln[l], out_scale)
        rep = lambda a: jnp.broadcast_to(a.astype(F32)[:, :, None], (nb, H_MLSTM, HEAD_DIM))
        h_m, c_s, n_s, m_s = _mlstm_step(lq, lk, lv, og, rep(gates[:, :H_MLSTM]), rep(gates[:, H_MLSTM:2 * H_MLSTM]),
                                         rep(state_mlstm_m[l]), state_mlstm_n[l].astype(F32),
                                         state_mlstm_C[l].astype(F32), g_mo)
        mem_o = _memory_attend_decode(mq, cache_mem_k.reshape(depth, nb, n_mem * H_MEM, HEAD_DIM),
                                      cache_mem_v.reshape(depth, nb, n_mem * H_MEM, HEAD_DIM), l)
        xs = _out_project(xs, diff_o, h_m.reshape(nb, D_MLSTM), mem_o, *_split_w_out(w_out[l], F32))
        xs = _ffn_half(xs, *ffn2)
        outs[7].append(dk.reshape(nb, 1, H_DIFF, 2, DH_HALF))
        outs[8].append(dv.reshape(nb, 1, H_DIFF, HEAD_DIM))
        outs[9].append(c_s)
        outs[10].append(n_s)
        outs[11].append(m_s[:, :, 0])

    stacked = [jnp.stack(o, 0) for o in outs]
    return (xp.reshape(1, s_len, d_model), xs.reshape(nb, 1, d_model), *stacked)
```

```python
import functools
import math

import jax
import jax.numpy as jnp
from jax import lax
from jax.experimental import pallas as pl
from jax.experimental.pallas import tpu as pltpu

F32 = jnp.float32
BF16 = jnp.bfloat16

HEAD_DIM = 128
H_DIFF = 8
DH_HALF = HEAD_DIM // 2
H_MLSTM = 4
H_MEM = 4
D_DIFF = H_DIFF * HEAD_DIM
D_MLSTM = H_MLSTM * HEAD_DIM
D_MEMH = H_MEM * HEAD_DIM
ROPE_THETA = 10000.0
EPS = 1e-6
NEG = -1e30
MLSTM_CHUNK = 128

LANES = 128
GATE_LANES = LANES
MIB = 1024 * 1024

ROW_TILE = 512
FF_TILE = 512
ATTN_TILE = 1024
DECODE_PAGES = 16

VMEM_SMALL = 32
VMEM_LARGE = 48
VMEM_PROJ = 56

_NT = (((1,), (1,)), ((), ()))


def _params(semantics, vmem_mib):
    return pltpu.CompilerParams(dimension_semantics=semantics, vmem_limit_bytes=vmem_mib * MIB)


def _row_tile(m, pref):
    return pref if m % pref == 0 else m


def _rms(x, g):
    return x * lax.rsqrt(jnp.mean(x * x, axis=-1, keepdims=True) + EPS) * g


def _split(x):
    hi = x.astype(BF16)
    return hi, (x - hi.astype(F32)).astype(BF16)


def _mm(x, w):
    if w.dtype == BF16:
        return jnp.dot(x.astype(BF16), w, preferred_element_type=F32)
    wh, wl = _split(w)
    if x.dtype == BF16:
        return jnp.dot(x, wl, preferred_element_type=F32) + jnp.dot(x, wh, preferred_element_type=F32)
    xh, xl = _split(x)
    m = x.shape[0]
    top = jnp.dot(jnp.concatenate([xh, xl], axis=0), wh, preferred_element_type=F32)
    return (top[m:] + jnp.dot(xh, wl, preferred_element_type=F32)) + top[:m]


def _resident(shape):
    return pl.BlockSpec(shape, lambda *_: (0,) * len(shape), pipeline_mode=pl.Buffered(1))


def _log_sigmoid(x):
    return jnp.minimum(x, 0.0) - jnp.log1p(jnp.exp(-jnp.abs(x)))


def _ffn_kernel(*refs, n_ff, next_norm):
    if next_norm:
        x_ref, g_ref, wg_ref, wu_ref, wd_ref, gn_ref, y_ref, hn_ref, h_scr = refs
    else:
        x_ref, g_ref, wg_ref, wu_ref, wd_ref, y_ref, h_scr = refs
    j = pl.program_id(1)

    @pl.when(j == 0)
    def _():
        h_scr[...] = _rms(x_ref[...], g_ref[...]).astype(h_scr.dtype)
        y_ref[...] = jnp.zeros(y_ref.shape, F32)

    h = h_scr[...]
    gate = _mm(h, wg_ref[...])
    up = _mm(h, wu_ref[...])
    y_ref[...] += _mm(gate * jax.nn.sigmoid(gate) * up, wd_ref[...])

    @pl.when(j == n_ff - 1)
    def _():
        y = x_ref[...] + 0.5 * y_ref[...]
        y_ref[...] = y
        if next_norm:
            hn_ref[...] = _rms(y, gn_ref[...]).astype(hn_ref.dtype)


def _ffn_half(x, g, wg, wu, wd, g_next=None):
    m, d = x.shape
    d_ff = wg.shape[1]
    tm = _row_tile(m, ROW_TILE)
    tf = _row_tile(d_ff, FF_TILE)
    n_ff = d_ff // tf
    next_norm = g_next is not None
    row = pl.BlockSpec((tm, d), lambda i, j: (i, 0))
    vec = pl.BlockSpec((1, d), lambda i, j: (0, 0))
    in_specs = [row, vec, pl.BlockSpec((d, tf), lambda i, j: (0, j)), pl.BlockSpec((d, tf), lambda i, j: (0, j)),
                pl.BlockSpec((tf, d), lambda i, j: (j, 0))]
    args = [x, g.reshape(1, d), wg, wu, wd]
    out_shape = [jax.ShapeDtypeStruct((m, d), F32)]
    out_specs = [row]
    if next_norm:
        in_specs.append(vec)
        args.append(g_next.reshape(1, d))
        out_shape.append(jax.ShapeDtypeStruct((m, d), wg.dtype))
        out_specs.append(row)
    out = pl.pallas_call(
        functools.partial(_ffn_kernel, n_ff=n_ff, next_norm=next_norm),
        grid=(m // tm, n_ff),
        in_specs=in_specs,
        out_specs=out_specs,
        out_shape=out_shape,
        scratch_shapes=[pltpu.VMEM((tm, d), wg.dtype)],
        compiler_params=_params(("parallel", "arbitrary"), VMEM_LARGE),
        name="ffn_half",
    )(*args)
    return out if next_norm else out[0]


def _diffproj_kernel(*refs, q_scale, prompt):
    if prompt:
        h_ref, wq_ref, wk_ref, wv_ref, gq_ref, gk_ref, cos_ref, sin_ref, seg_ref, wvt_ref, qb_ref, kf_ref, vf_ref, kb_ref, vt_ref = refs
    else:
        h_ref, wq_ref, wk_ref, wv_ref, gq_ref, gk_ref, cos_ref, sin_ref, seg_ref, qb_ref, kf_ref, vf_ref = refs
    h = h_ref[...]
    tm = h.shape[0]
    heads = qb_ref.shape[1] // LANES
    cos = cos_ref[...]
    sin = sin_ref[...]
    seg = seg_ref[...]
    lane = lax.broadcasted_iota(jnp.int32, (tm, LANES), 1)
    first_half = (lane & (DH_HALF - 1)) < (DH_HALF // 2)

    def norm_rope(z, g):
        ss = _mm(z * z, seg)
        y = z * lax.rsqrt(ss * (1.0 / DH_HALF) + EPS) * g
        partner = jnp.where(first_half, pltpu.roll(y, LANES - DH_HALF // 2, axis=1), pltpu.roll(y, DH_HALF // 2, axis=1))
        return y * cos + partner * sin

    zq = _mm(h, wq_ref[...])
    gq = gq_ref[...]
    for t in range(heads):
        sl = slice(t * LANES, (t + 1) * LANES)
        qb_ref[:, sl] = (norm_rope(zq[:, sl], gq) * q_scale).astype(BF16)
    zk = _mm(h, wk_ref[...])
    gk = gk_ref[...]
    for t in range(heads):
        sl = slice(t * LANES, (t + 1) * LANES)
        kt = norm_rope(zk[:, sl], gk)
        kf_ref[:, sl] = kt
        if prompt:
            kb_ref[:, sl] = kt.astype(BF16)
    vf_ref[...] = _mm(h, wv_ref[...])
    if prompt:
        vt_ref[0] = lax.dot_general(wvt_ref[...], h, _NT, preferred_element_type=F32).astype(BF16)


def _diff_project(h, wq, wk, wv, g_q, g_k, cos, sin, q_scale, wv_t=None):
    m, d = h.shape
    tm = _row_tile(m, ROW_TILE)
    prompt = wv_t is not None
    tn = D_DIFF if wq.dtype == BF16 else 2 * LANES
    seg = jnp.kron(jnp.eye(2, dtype=F32), jnp.ones((DH_HALF, DH_HALF), F32)).astype(wq.dtype)
    row_in = pl.BlockSpec((tm, d), lambda i, j: (i, 0))
    w_spec = _resident((d, tn)) if tn == D_DIFF else pl.BlockSpec((d, tn), lambda i, j: (0, j))
    vec = pl.BlockSpec((1, LANES), lambda i, j: (0, 0))
    tab = pl.BlockSpec((tm, LANES), lambda i, j: (i, 0))
    row_out = pl.BlockSpec((tm, tn), lambda i, j: (i, j))
    in_specs = [row_in, w_spec, w_spec, w_spec, vec, vec, tab, tab, pl.BlockSpec((LANES, LANES), lambda i, j: (0, 0))]
    args = [h, wq, wk, wv, jnp.tile(g_q, 2).reshape(1, LANES), jnp.tile(g_k, 2).reshape(1, LANES), cos, sin, seg]
    out_specs = [row_out] * 3
    out_shape = [jax.ShapeDtypeStruct((m, D_DIFF), dt) for dt in (BF16, F32, F32)]
    if prompt:
        assert tn == D_DIFF
        in_specs.append(_resident((D_DIFF, d)))
        args.append(wv_t)
        out_specs += [row_out, pl.BlockSpec((1, D_DIFF, tm), lambda i, j: (i, 0, 0))]
        out_shape += [jax.ShapeDtypeStruct((m, D_DIFF), BF16), jax.ShapeDtypeStruct((m // tm, D_DIFF, tm), BF16)]
    return pl.pallas_call(
        functools.partial(_diffproj_kernel, q_scale=q_scale, prompt=prompt),
        grid=(m // tm, D_DIFF // tn),
        in_specs=in_specs,
        out_specs=out_specs,
        out_shape=out_shape,
        compiler_params=_params(("parallel", "arbitrary"), VMEM_PROJ),
        name="diff_project",
    )(*args)


def _mlstmproj_kernel(h_ref, wq_ref, wk_ref, wv_ref, wo_ref, wg_ref, b_ref, q_ref, k_ref, v_ref, og_ref, gt_ref, *,
                      transpose_gates):
    h = h_ref[...]
    tm = h.shape[0]
    q_ref[...] = _mm(h, wq_ref[...]).astype(q_ref.dtype)
    k_ref[...] = (_mm(h, wk_ref[...]) * (HEAD_DIM ** -0.5)).astype(k_ref.dtype)
    v_ref[...] = _mm(h, wv_ref[...]).astype(v_ref.dtype)
    og_ref[...] = jax.nn.sigmoid(_mm(h, wo_ref[...]))
    zg = _mm(h, wg_ref[...]) + b_ref[...]
    lane = lax.broadcasted_iota(jnp.int32, (tm, GATE_LANES), 1)
    gates = jnp.where(lane < H_MLSTM, zg, _log_sigmoid(zg))
    if transpose_gates:
        for c in range(tm // MLSTM_CHUNK):
            gt = jnp.transpose(gates[c * MLSTM_CHUNK:(c + 1) * MLSTM_CHUNK, :])
            gt_ref[c] = gt[:2 * H_MLSTM, :]
    else:
        gt_ref[...] = gates


def _mlstm_project(h, w, w_gate, bias, transpose_gates):
    m, d = h.shape
    dt = w[0].dtype
    tm = _row_tile(m, ROW_TILE)
    row_in = pl.BlockSpec((tm, d), lambda i: (i, 0))
    row_out = pl.BlockSpec((tm, D_MLSTM), lambda i: (i, 0))
    if transpose_gates:
        cpt = tm // MLSTM_CHUNK
        g_shape = jax.ShapeDtypeStruct((m // MLSTM_CHUNK, 2 * H_MLSTM, MLSTM_CHUNK), F32)
        g_spec = pl.BlockSpec((cpt, 2 * H_MLSTM, MLSTM_CHUNK), lambda i: (i, 0, 0))
    else:
        g_shape = jax.ShapeDtypeStruct((m, GATE_LANES), F32)
        g_spec = pl.BlockSpec((tm, GATE_LANES), lambda i: (i, 0))
    return pl.pallas_call(
        functools.partial(_mlstmproj_kernel, transpose_gates=transpose_gates),
        grid=(m // tm,),
        in_specs=[row_in] + [_resident((d, D_MLSTM))] * 4 + [_resident((d, GATE_LANES)), _resident((1, GATE_LANES))],
        out_specs=[row_out, row_out, row_out, row_out, g_spec],
        out_shape=[jax.ShapeDtypeStruct((m, D_MLSTM), t) for t in (dt, dt, dt, F32)] + [g_shape],
        compiler_params=_params(("parallel",), VMEM_LARGE),
        name="mlstm_project",
    )(h, *w, w_gate, bias)


def _softmax_pv(s, v):
    m = jnp.max(s, axis=-1, keepdims=True)
    p = jnp.exp(s - m)
    l = jnp.sum(p, axis=-1, keepdims=True)
    return jnp.dot(p.astype(BF16), v, preferred_element_type=F32) / l


def _memq_kernel(*refs, attend):
    if attend:
        h_ref, w_ref, gq_ref, mk_ref, mv_ref, o_ref = refs
    else:
        h_ref, w_ref, gq_ref, o_ref = refs
    z = _mm(h_ref[...], w_ref[...])
    gq = gq_ref[...]
    for t in range(H_MEM):
        sl = slice(t * HEAD_DIM, (t + 1) * HEAD_DIM)
        q = _rms(z[:, sl], gq)
        if attend:
            s = lax.dot_general(q.astype(BF16), mk_ref[:, sl], _NT, preferred_element_type=F32) * (HEAD_DIM ** -0.5)
            q = _softmax_pv(s, mv_ref[:, sl])
        o_ref[:, sl] = q.astype(BF16)


def _mem_query(h, w, g_q, mem_k=None, mem_v=None):
    m, d = h.shape
    tm = _row_tile(m, ROW_TILE)
    attend = mem_k is not None
    in_specs = [pl.BlockSpec((tm, d), lambda i: (i, 0)), pl.BlockSpec((d, D_MEMH), lambda i: (0, 0)),
                pl.BlockSpec((1, HEAD_DIM), lambda i: (0, 0))]
    args = [h, w, g_q.reshape(1, HEAD_DIM)]
    if attend:
        n_mem = mem_k.shape[0]
        in_specs += [pl.BlockSpec((n_mem, D_MEMH), lambda i: (0, 0))] * 2
        args += [mem_k, mem_v]
    return pl.pallas_call(
        functools.partial(_memq_kernel, attend=attend),
        grid=(m // tm,),
        in_specs=in_specs,
        out_specs=pl.BlockSpec((tm, D_MEMH), lambda i: (i, 0)),
        out_shape=jax.ShapeDtypeStruct((m, D_MEMH), BF16),
        compiler_params=_params(("parallel",), VMEM_SMALL),
        name="mem_query",
    )(*args)


def _memkv_kernel(mem_ref, g_ref, wk_ref, wv_ref, gk_ref, kf_ref, vf_ref, kb_ref, vb_ref):
    mn = _rms(mem_ref[...], g_ref[...]).astype(BF16)
    zk = jnp.dot(mn, wk_ref[...], preferred_element_type=F32)
    gk = gk_ref[...]
    for t in range(H_MEM):
        sl = slice(t * HEAD_DIM, (t + 1) * HEAD_DIM)
        kt = _rms(zk[:, sl], gk)
        kf_ref[:, sl] = kt
        kb_ref[:, sl] = kt.astype(BF16)
    zv = jnp.dot(mn, wv_ref[...], preferred_element_type=F32)
    vf_ref[...] = zv
    vb_ref[...] = zv.astype(BF16)


def _memory_kv(mem, g_in, wk, wv, g_k):
    n_mem, d = mem.shape
    shapes = [jax.ShapeDtypeStruct((n_mem, D_MEMH), dt) for dt in (F32, F32, BF16, BF16)]
    return pl.pallas_call(
        _memkv_kernel,
        out_shape=shapes,
        compiler_params=pltpu.CompilerParams(vmem_limit_bytes=VMEM_SMALL * MIB),
        name="memory_kv",
    )(mem, g_in.reshape(1, d), wk, wv, g_k.reshape(1, HEAD_DIM))


def _lambda_kernel(q1_ref, k1_ref, q2_ref, k2_ref, o_ref, *, lambda_init):
    a = jnp.exp(jnp.sum(q1_ref[...] * k1_ref[...], axis=-1, keepdims=True))
    b = jnp.exp(jnp.sum(q2_ref[...] * k2_ref[...], axis=-1, keepdims=True))
    o_ref[...] = a - b + lambda_init


def _diff_lambda(q1, k1, q2, k2, lambda_init):
    r = lambda a: a.reshape(1, DH_HALF).astype(F32)
    return pl.pallas_call(
        functools.partial(_lambda_kernel, lambda_init=lambda_init),
        out_shape=jax.ShapeDtypeStruct((1, 1), F32),
        name="diff_lambda",
    )(r(q1), r(k1), r(q2), r(k2))


def _split_maps(q):
    lane = lax.broadcasted_iota(jnp.int32, q.shape, 1)
    qf = q.astype(F32)
    return jnp.concatenate([jnp.where(lane < DH_HALF, qf, 0.0), jnp.where(lane >= DH_HALF, qf, 0.0)], axis=0).astype(BF16)


def _subln(d, g, out_scale):
    return _rms(d, g) * out_scale


def _attn_kernel(lam_ref, q_ref, k_ref, vt_ref, gs_ref, o_ref, m_scr, l_scr, acc_scr, *, tq, out_scale):
    i = pl.program_id(1)
    qs = _split_maps(q_ref[...])
    m_scr[...] = jnp.full(m_scr.shape, NEG, F32)
    l_scr[...] = jnp.zeros(l_scr.shape, F32)
    acc_scr[...] = jnp.zeros(acc_scr.shape, F32)

    def step(j, masked):
        kj = k_ref[pl.ds(pl.multiple_of(j * tq, tq), tq), :]
        s = lax.dot_general(kj, qs, _NT, preferred_element_type=F32)
        if masked:
            key = lax.broadcasted_iota(jnp.int32, s.shape, 0)
            qry = lax.broadcasted_iota(jnp.int32, s.shape, 1)
            s = jnp.where(key <= jnp.where(qry >= tq, qry - tq, qry), s, NEG)
        m_old = m_scr[...]
        m_new = jnp.maximum(m_old, jnp.max(s, axis=0, keepdims=True))
        alpha = jnp.exp2(m_old - m_new)
        p = jnp.exp2(s - m_new)
        l_scr[...] = alpha * l_scr[...] + jnp.sum(p, axis=0, keepdims=True)
        pb = p.astype(BF16)
        tv = vt_ref.shape[2]
        pv = jnp.dot(vt_ref[j * (tq // tv)], pb[:tv], preferred_element_type=F32)
        for u in range(1, tq // tv):
            pv += jnp.dot(vt_ref[j * (tq // tv) + u], pb[u * tv:(u + 1) * tv], preferred_element_type=F32)
        acc_scr[...] = alpha * acc_scr[...] + pv
        m_scr[...] = m_new

    def body(j, carry):
        step(j, False)
        return carry

    lax.fori_loop(0, i, body, 0)
    step(i, True)

    o_t = acc_scr[...] / l_scr[...]
    d = jnp.transpose(o_t[:, :tq] - lam_ref[0, 0] * o_t[:, tq:])
    o_ref[...] = _subln(d, gs_ref[...], out_scale).astype(BF16)


def _diff_attention_prompt(lam, q, k, v_t, g_subln, out_scale):
    s_len = q.shape[0]
    tv = v_t.shape[2]
    tq = _row_tile(s_len, ATTN_TILE)
    assert tq % tv == 0
    blk = pl.BlockSpec((tq, HEAD_DIM), lambda h, i: (i, h))
    return pl.pallas_call(
        functools.partial(_attn_kernel, tq=tq, out_scale=out_scale),
        grid=(H_DIFF, s_len // tq),
        in_specs=[pl.BlockSpec(memory_space=pltpu.SMEM), blk, pl.BlockSpec((s_len, HEAD_DIM), lambda h, i: (0, h)),
                  pl.BlockSpec((s_len // tv, HEAD_DIM, tv), lambda h, i: (0, h, 0)),
                  pl.BlockSpec((1, HEAD_DIM), lambda h, i: (0, 0))],
        out_specs=blk,
        out_shape=jax.ShapeDtypeStruct((s_len, D_DIFF), BF16),
        scratch_shapes=[pltpu.VMEM((1, 2 * tq), F32), pltpu.VMEM((1, 2 * tq), F32), pltpu.VMEM((HEAD_DIM, 2 * tq), F32)],
        compiler_params=_params(("parallel", "arbitrary"), VMEM_LARGE),
        name="diff_attention_prompt",
    )(lam, q, k, v_t, g_subln.reshape(1, HEAD_DIM))


def _decode_attn_kernel(pt_ref, lam_ref, q_ref, kn_ref, vn_ref, gs_ref, *refs, pages, out_scale):
    k_refs = refs[:pages]
    v_refs = refs[pages:2 * pages]
    o_ref, qs_scr, m_scr, l_scr, acc_scr = refs[2 * pages:]
    p_idx = pl.program_id(1)
    n_maps = 2 * H_DIFF
    page = k_refs[0].shape[1]

    @pl.when(p_idx == 0)
    def _():
        row = lax.broadcasted_iota(jnp.int32, (n_maps, D_DIFF), 0)
        col = lax.broadcasted_iota(jnp.int32, (n_maps, D_DIFF), 1)
        qb = jnp.broadcast_to(q_ref[0].astype(F32), (n_maps, D_DIFF))
        qs_scr[...] = jnp.where(col // DH_HALF == row, qb, 0.0).astype(BF16)
        m_scr[...] = jnp.full(m_scr.shape, NEG, F32)
        l_scr[...] = jnp.zeros(l_scr.shape, F32)
        acc_scr[...] = jnp.zeros(acc_scr.shape, F32)

    qs = qs_scr[...]
    s = jnp.concatenate(
        [lax.dot_general(qs, k_refs[t][0], _NT, preferred_element_type=F32) for t in range(pages)], axis=1)
    m_old = m_scr[...]
    m_new = jnp.maximum(m_old, jnp.max(s, axis=-1, keepdims=True))
    alpha = jnp.exp(m_old - m_new)
    p = jnp.exp(s - m_new).astype(BF16)
    l_scr[...] = alpha * l_scr[...] + jnp.sum(p.astype(F32), axis=-1, keepdims=True)
    for h in range(H_DIFF):
        sl = slice(h * HEAD_DIM, (h + 1) * HEAD_DIM)
        rows = pl.ds(h, page, stride=H_DIFF)
        pv = jnp.dot(p[:, :page], v_refs[0][0, 0, rows, :].astype(BF16), preferred_element_type=F32)
        for t in range(1, pages):
            pv += jnp.dot(p[:, t * page:(t + 1) * page], v_refs[t][0, 0, rows, :].astype(BF16),
                          preferred_element_type=F32)
        acc_scr[:, sl] = alpha * acc_scr[:, sl] + pv
    m_scr[...] = m_new

    @pl.when(p_idx == pl.num_programs(1) - 1)
    def _():
        s_new = jnp.sum(qs.astype(F32) * kn_ref[0], axis=-1, keepdims=True)
        m_fin = jnp.maximum(m_new, s_new)
        a_fin = jnp.exp(m_new - m_fin)
        p_new = jnp.exp(s_new - m_fin)
        l_fin = a_fin * l_scr[...] + p_new
        o = (a_fin * acc_scr[...] + p_new * vn_ref[0]) / l_fin
        lam = lam_ref[0, 0]
        gs = gs_ref[...]
        for h in range(H_DIFF):
            sl = slice(h * HEAD_DIM, (h + 1) * HEAD_DIM)
            d = o[2 * h:2 * h + 1, sl] - lam * o[2 * h + 1:2 * h + 2, sl]
            o_ref[0, :, sl] = _subln(d, gs, out_scale).astype(BF16)


def _diff_attention_decode(lam, q, k_new, v_new, cache_k, cache_v, layer, page_table, g_subln, out_scale,
                           pages_per_step=DECODE_PAGES):
    nb = q.shape[0]
    n_pages = page_table.shape[1]
    n_maps = 2 * H_DIFF
    page = cache_k.shape[1]
    pages = pages_per_step if n_pages % pages_per_step == 0 else 1
    tok = pl.BlockSpec((1, 1, D_DIFF), lambda b, p, pt: (b, 0, 0))

    def k_spec(t):
        return pl.BlockSpec((1, page, D_DIFF), lambda b, p, pt: (pt[b, p * pages + t], 0, 0))

    def v_spec(t):
        return pl.BlockSpec((1, 1, page * H_DIFF, HEAD_DIM), lambda b, p, pt: (layer, pt[b, p * pages + t], 0, 0))

    grid_spec = pltpu.PrefetchScalarGridSpec(
        num_scalar_prefetch=1,
        grid=(nb, n_pages // pages),
        in_specs=[pl.BlockSpec(memory_space=pltpu.SMEM), tok, tok, tok, pl.BlockSpec((1, HEAD_DIM), lambda b, p, pt: (0, 0))]
        + [k_spec(t) for t in range(pages)] + [v_spec(t) for t in range(pages)],
        out_specs=tok,
        scratch_shapes=[pltpu.VMEM((n_maps, D_DIFF), BF16), pltpu.VMEM((n_maps, 1), F32), pltpu.VMEM((n_maps, 1), F32),
                        pltpu.VMEM((n_maps, D_DIFF), F32)],
    )
    out = pl.pallas_call(
        functools.partial(_decode_attn_kernel, pages=pages, out_scale=out_scale),
        grid_spec=grid_spec,
        out_shape=jax.ShapeDtypeStruct((nb, 1, D_DIFF), BF16),
        compiler_params=_params(("parallel", "arbitrary"), VMEM_LARGE),
        name="diff_attention_decode",
    )(page_table, lam, q.reshape(nb, 1, D_DIFF), k_new.reshape(nb, 1, D_DIFF), v_new.reshape(nb, 1, D_DIFF),
      g_subln.reshape(1, HEAD_DIM), *([cache_k] * pages), *([cache_v] * pages))
    return out.reshape(nb, D_DIFF)


def _mlstm_kernel(q_ref, k_ref, v_ref, og_ref, gt_ref, go_ref, hm_ref, c_ref, n_ref, m_ref):
    @pl.when(pl.program_id(0) == 0)
    def _():
        c_ref[...] = jnp.zeros(c_ref.shape, F32)
        n_ref[...] = jnp.zeros(n_ref.shape, F32)
        m_ref[...] = jnp.zeros(m_ref.shape, F32)

    L = MLSTM_CHUNK
    lane8 = lax.broadcasted_iota(jnp.int32, (2 * H_MLSTM, L), 1)
    row = lax.broadcasted_iota(jnp.int32, (L, L), 0)
    col = lax.broadcasted_iota(jnp.int32, (L, L), 1)
    causal = col <= row

    def chunk(c, carry):
        r0 = pl.multiple_of(c * L, L)
        g8 = gt_ref[c]
        cs = g8
        d = 1
        while d < L:
            cs = cs + jnp.where(lane8 >= d, pltpu.roll(cs, d, axis=1), 0.0)
            d *= 2
        stacked = jnp.concatenate([g8, cs, jnp.zeros((L - 4 * H_MLSTM, L), F32)], axis=0)
        cols = jnp.transpose(stacked)
        for h in range(H_MLSTM):
            sl = slice(h * HEAD_DIM, (h + 1) * HEAD_DIM)
            q = q_ref[pl.ds(r0, L), sl]
            k = k_ref[pl.ds(r0, L), sl]
            v = v_ref[pl.ds(r0, L), sl]
            ig_r = g8[h:h + 1, :]
            bt_r = cs[H_MLSTM + h:H_MLSTM + h + 1, :]
            ig_c = cols[:, h:h + 1]
            bt_c = cols[:, 3 * H_MLSTM + h:3 * H_MLSTM + h + 1]
            m0 = m_ref[h:h + 1, 0:1]
            c0 = c_ref[h]
            n0 = n_ref[h:h + 1, :]

            dm = jnp.where(causal, bt_c + (ig_r - bt_r), NEG)
            inter = bt_c + m0
            m_c = jnp.maximum(inter, jnp.max(dm, axis=-1, keepdims=True))
            w_intra = jnp.exp(dm - m_c)
            w_inter = jnp.exp(inter - m_c)
            a = w_intra * lax.dot_general(q, k, _NT, preferred_element_type=F32)
            cq = lax.dot_general(q, c0.astype(BF16), _NT, preferred_element_type=F32)
            num = jnp.dot(a.astype(BF16), v, preferred_element_type=F32) + w_inter * cq
            nq = jnp.sum(q.astype(F32) * n0, axis=-1, keepdims=True)
            den = jnp.sum(a, axis=-1, keepdims=True) + w_inter * nq
            hh = num / jnp.maximum(jnp.abs(den), jnp.exp(-m_c))
            hm_ref[pl.ds(r0, L), sl] = (_rms(hh, go_ref[:, sl]) * og_ref[pl.ds(r0, L), sl]).astype(BF16)

            bl = bt_r[:, L - 1:L]
            ml = m_c[L - 1:L, :]
            w_end = jnp.exp(bl - bt_c + ig_c - ml)
            decay = jnp.exp(bl + m0 - ml)
            vw_t = jnp.transpose(w_end * v.astype(F32)).astype(BF16)
            c_ref[h] = decay * c0 + jnp.dot(vw_t, k, preferred_element_type=F32)
            n_ref[h:h + 1, :] = decay * n0 + jnp.sum(w_end * k.astype(F32), axis=0, keepdims=True)
            m_ref[h:h + 1, :] = jnp.broadcast_to(ml, (1, LANES))
        return carry

    lax.fori_loop(0, gt_ref.shape[0], chunk, 0)


def _mlstm_prompt(q, k, v, og, gates_t, g_out):
    s_len = q.shape[0]
    tm = _row_tile(s_len, ROW_TILE)
    cpt = tm // MLSTM_CHUNK
    row = pl.BlockSpec((tm, D_MLSTM), lambda t: (t, 0))
    return pl.pallas_call(
        _mlstm_kernel,
        grid=(s_len // tm,),
        in_specs=[row, row, row, row, pl.BlockSpec((cpt, 2 * H_MLSTM, MLSTM_CHUNK), lambda t: (t, 0, 0)),
                  pl.BlockSpec((1, D_MLSTM), lambda t: (0, 0))],
        out_specs=[row, pl.BlockSpec((H_MLSTM, HEAD_DIM, HEAD_DIM), lambda t: (0, 0, 0)),
                   pl.BlockSpec((H_MLSTM, HEAD_DIM), lambda t: (0, 0)), pl.BlockSpec((H_MLSTM, LANES), lambda t: (0, 0))],
        out_shape=[jax.ShapeDtypeStruct((s_len, D_MLSTM), BF16), jax.ShapeDtypeStruct((H_MLSTM, HEAD_DIM, HEAD_DIM), F32),
                   jax.ShapeDtypeStruct((H_MLSTM, HEAD_DIM), F32), jax.ShapeDtypeStruct((H_MLSTM, LANES), F32)],
        compiler_params=_params(("arbitrary",), VMEM_SMALL),
        name="mlstm_prompt",
    )(q, k, v, og, gates_t, g_out.reshape(1, D_MLSTM))


def _mlstm_step_kernel(q_ref, k_ref, v_ref, og_ref, ig_ref, lf_ref, m0_ref, n0_ref, c0_ref, go_ref,
                       hm_ref, c1_ref, n1_ref, m1_ref):
    q = q_ref[0].astype(F32)
    k = k_ref[0].astype(F32)
    v = v_ref[0].astype(F32)
    ig = ig_ref[0]
    lf = lf_ref[0]
    m0 = m0_ref[0]
    n0 = n0_ref[0]
    m1 = jnp.maximum(lf + m0, ig)
    w_i = jnp.exp(ig - m1)
    w_f = jnp.exp(lf + m0 - m1)
    a = w_i * jnp.sum(q * k, axis=-1, keepdims=True)
    cq_rows = []
    for h in range(H_MLSTM):
        qh = jnp.broadcast_to(q[h:h + 1, :], (8, HEAD_DIM)).astype(BF16)
        cq = lax.dot_general(qh, c0_ref[0, h].astype(BF16), _NT, preferred_element_type=F32)
        cq_rows.append(cq[0:1, :])
    cq = jnp.concatenate(cq_rows, axis=0)
    num = a * v + w_f * cq
    den = a + w_f * jnp.sum(n0 * q, axis=-1, keepdims=True)
    hh = num / jnp.maximum(jnp.abs(den), jnp.exp(-m1))
    hm_ref[0] = (_rms(hh, go_ref[...]) * og_ref[0]).astype(BF16)
    n1_ref[0] = w_f * n0 + w_i * k
    m1_ref[0] = m1
    wv = w_i * v
    wv_cols = jnp.transpose(jnp.concatenate([wv, jnp.zeros((HEAD_DIM - H_MLSTM, HEAD_DIM), F32)], axis=0))
    for h in range(H_MLSTM):
        c1_ref[0, h] = w_f[h:h + 1, 0:1] * c0_ref[0, h] + wv_cols[:, h:h + 1] * k[h:h + 1, :]


def _mlstm_step(q, k, v, og, ig, lf, m0, n0, c0, g_out):
    nb = q.shape[0]
    hd = (nb, H_MLSTM, HEAD_DIM)
    tok = pl.BlockSpec((1, H_MLSTM, HEAD_DIM), lambda b: (b, 0, 0))
    mat = pl.BlockSpec((1, H_MLSTM, HEAD_DIM, HEAD_DIM), lambda b: (b, 0, 0, 0))
    return pl.pallas_call(
        _mlstm_step_kernel,
        grid=(nb,),
        in_specs=[tok] * 8 + [mat, pl.BlockSpec((H_MLSTM, HEAD_DIM), lambda b: (0, 0))],
        out_specs=[tok, mat, tok, tok],
        out_shape=[jax.ShapeDtypeStruct(hd, BF16), jax.ShapeDtypeStruct((nb, H_MLSTM, HEAD_DIM, HEAD_DIM), F32),
                   jax.ShapeDtypeStruct(hd, F32), jax.ShapeDtypeStruct(hd, F32)],
        compiler_params=_params(("parallel",), VMEM_SMALL),
        name="mlstm_step",
    )(q.reshape(hd), k.reshape(hd), v.reshape(hd), og.reshape(hd), ig, lf, m0, n0, c0, g_out)


def _mem_decode_kernel(q_ref, k_ref, v_ref, o_ref):
    n_mem = k_ref.shape[2] // H_MEM
    for t in range(H_MEM):
        sl = slice(t * HEAD_DIM, (t + 1) * HEAD_DIM)
        rows = pl.ds(t, n_mem, stride=H_MEM)
        q = jnp.broadcast_to(q_ref[0, :, sl], (8, HEAD_DIM))
        s = lax.dot_general(q, k_ref[0, 0, rows, :].astype(BF16), _NT, preferred_element_type=F32) * (HEAD_DIM ** -0.5)
        o = _softmax_pv(s, v_ref[0, 0, rows, :].astype(BF16))
        o_ref[0, :, sl] = o[0:1, :].astype(BF16)


def _memory_attend_decode(q, mem_k, mem_v, layer):
    _, nb, rows, _ = mem_k.shape
    tok = pl.BlockSpec((1, 1, D_MEMH), lambda b: (b, 0, 0))
    mem = pl.BlockSpec((1, 1, rows, HEAD_DIM), lambda b: (layer, b, 0, 0))
    out = pl.pallas_call(
        _mem_decode_kernel,
        grid=(nb,),
        in_specs=[tok, mem, mem],
        out_specs=tok,
        out_shape=jax.ShapeDtypeStruct((nb, 1, D_MEMH), BF16),
        compiler_params=_params(("parallel",), VMEM_SMALL),
        name="memory_attend_decode",
    )(q.reshape(nb, 1, D_MEMH), mem_k, mem_v)
    return out.reshape(nb, D_MEMH)


def _outproj_kernel(x_ref, d_ref, m_ref, c_ref, wd_ref, wm_ref, wc_ref, y_ref):
    y = x_ref[...] + _mm(d_ref[...], wd_ref[...])
    y += _mm(m_ref[...], wm_ref[...])
    y += _mm(c_ref[...], wc_ref[...])
    y_ref[...] = y


def _out_project(x, d, m, c, w_d, w_m, w_c):
    rows, dm = x.shape
    tm = _row_tile(rows, ROW_TILE)

    def row(width):
        return pl.BlockSpec((tm, width), lambda i: (i, 0))

    return pl.pallas_call(
        _outproj_kernel,
        grid=(rows // tm,),
        in_specs=[row(dm), row(D_DIFF), row(D_MLSTM), row(D_MEMH), _resident(w_d.shape), _resident(w_m.shape),
                  _resident(w_c.shape)],
        out_specs=row(dm),
        out_shape=jax.ShapeDtypeStruct((rows, dm), F32),
        compiler_params=_params(("parallel",), VMEM_LARGE),
        name="out_project",
    )(x, d, m, c, w_d, w_m, w_c)


def _rope_tables(pos):
    inv = 1.0 / (ROPE_THETA ** (jnp.arange(0, DH_HALF, 2, dtype=F32) / DH_HALF))
    ang = pos.astype(F32)[:, None] * inv[None, :]
    c, s = jnp.cos(ang), jnp.sin(ang)
    return jnp.tile(c, (1, 4)), jnp.concatenate([-s, s, -s, s], axis=1)


def _split_w_in(w_in, dtype):
    o = 0
    wq, wk, wv = (w_in[:, o + t * D_DIFF:o + (t + 1) * D_DIFF].astype(dtype) for t in range(3))
    o = 3 * D_DIFF
    w_ml = tuple(w_in[:, o + t * D_MLSTM:o + (t + 1) * D_MLSTM].astype(dtype) for t in range(4))
    o += 4 * D_MLSTM
    w_gate = jnp.pad(w_in[:, o:o + 2 * H_MLSTM], ((0, 0), (0, GATE_LANES - 2 * H_MLSTM))).astype(dtype)
    o += 2 * H_MLSTM
    w_mq = w_in[:, o:o + D_MEMH].astype(dtype)
    return wq, wk, wv, w_ml, w_gate, w_mq


def _split_w_out(w_out, dtype):
    w = w_out.astype(dtype)
    return w[:D_DIFF], w[D_DIFF:D_DIFF + D_MLSTM], w[D_DIFF + D_MLSTM:]


def kernel(x_prompt, x_sample, cache_diff_k, cache_diff_v, cache_mem_k, cache_mem_v, state_mlstm_C, state_mlstm_n,
           state_mlstm_m, page_table, mem_prompt, g_ffn1, w_ffn1_gate, w_ffn1_up, w_ffn1_down, g_mix, w_in, b_igate,
           b_fgate, g_q_diff, g_k_diff, lambda_q1, lambda_k1, lambda_q2, lambda_k2, g_subln, g_mlstm_out, g_mem_in,
           w_mem_k, w_mem_v, g_q_mem, g_k_mem, w_out, g_ffn2, w_ffn2_gate, w_ffn2_up, w_ffn2_down):
    depth = w_in.shape[0]
    bp, s_len, d_model = x_prompt.shape
    nb, t_s, _ = x_sample.shape
    assert bp == 1 and t_s == 1, "one prompt sequence and one new token per decode request"
    n_phys, page = cache_diff_k.shape[1], cache_diff_k.shape[2]
    n_past = page_table.shape[1] * page
    n_mem = mem_prompt.shape[1]

    xp = x_prompt.reshape(s_len, d_model)
    xs = x_sample.reshape(nb, d_model)
    cos_p, sin_p = _rope_tables(jnp.arange(s_len))
    cos_s, sin_s = (jnp.broadcast_to(t, (nb, LANES)) for t in _rope_tables(jnp.full((1,), n_past)))

    outs = [[] for _ in range(12)]
    for l in range(depth):
        lambda_init = 0.8 - 0.6 * math.exp(-0.3 * l)
        out_scale = 1.0 - lambda_init
        bf = lambda a: a[l].astype(BF16)
        ffn1 = (g_ffn1[l], bf(w_ffn1_gate), bf(w_ffn1_up), bf(w_ffn1_down))
        ffn2 = (g_ffn2[l], bf(w_ffn2_gate), bf(w_ffn2_up), bf(w_ffn2_down))
        wq, wk, wv, w_ml, w_gate, w_mq = _split_w_in(w_in[l], BF16)
        gate_bias = jnp.pad(jnp.concatenate([b_igate[l], b_fgate[l]]).astype(F32),
                            (0, GATE_LANES - 2 * H_MLSTM)).reshape(1, GATE_LANES)
        lam = _diff_lambda(lambda_q1[l], lambda_k1[l], lambda_q2[l], lambda_k2[l], lambda_init)
        g_mo = g_mlstm_out[l].reshape(H_MLSTM, HEAD_DIM)

        xp, hp = _ffn_half(xp, *ffn1, g_next=g_mix[l])
        dq, dk, dv, dk_b, dv_t = _diff_project(hp, wq, wk, wv, g_q_diff[l], g_k_diff[l], cos_p, sin_p,
                                               DH_HALF ** -0.5 * math.log2(math.e), wv_t=wv.T)
        lq, lk, lv, og, gates_t = _mlstm_project(hp, w_ml, w_gate, gate_bias, transpose_gates=True)
        mk, mv, mk_b, mv_b = _memory_kv(mem_prompt.reshape(n_mem, d_model), g_mem_in[l], bf(w_mem_k), bf(w_mem_v),
                                        g_k_mem[l])
        mem_o = _mem_query(hp, w_mq, g_q_mem[l], mk_b, mv_b)
        diff_o = _diff_attention_prompt(lam, dq, dk_b, dv_t, g_subln[l], out_scale)
        h_m, c_p, n_p, m_p = _mlstm_prompt(lq, lk, lv, og, gates_t, g_mo)
        xp = _out_project(xp, diff_o, h_m, mem_o, *_split_w_out(w_out[l], BF16))
        xp = _ffn_half(xp, *ffn2)
        outs[0].append(dk.reshape(1, s_len, H_DIFF, 2, DH_HALF))
        outs[1].append(dv.reshape(1, s_len, H_DIFF, HEAD_DIM))
        outs[2].append(mk.reshape(1, n_mem, H_MEM, HEAD_DIM))
        outs[3].append(mv.reshape(1, n_mem, H_MEM, HEAD_DIM))
        outs[4].append(c_p.reshape(1, H_MLSTM, HEAD_DIM, HEAD_DIM))
        outs[5].append(n_p.reshape(1, H_MLSTM, HEAD_DIM))
        outs[6].append(m_p[:, 0].reshape(1, H_MLSTM))

        ffn1 = (g_ffn1[l], w_ffn1_gate[l], w_ffn1_up[l], w_ffn1_down[l])
        ffn2 = (g_ffn2[l], w_ffn2_gate[l], w_ffn2_up[l], w_ffn2_down[l])
        wq, wk, wv, w_ml, w_gate, w_mq = _split_w_in(w_in[l], F32)
        xs, hs = _ffn_half(xs, *ffn1, g_next=g_mix[l])
        dq, dk, dv = _diff_project(hs, wq, wk, wv, g_q_diff[l], g_k_diff[l], cos_s, sin_s, DH_HALF ** -0.5)
        lq, lk, lv, og, gates = _mlstm_project(hs, w_ml, w_gate, gate_bias, transpose_gates=False)
        mq = _mem_query(hs, w_mq, g_q_mem[l])
        diff_o = _diff_attention_decode(lam, dq, dk, dv, cache_diff_k[l].astype(BF16).reshape(n_phys, page, D_DIFF),
                                        cache_diff_v.reshape(depth, n_phys, page * H_DIFF, HEAD_DIM), l, page_table,
                                        g_subln[l], out_scale)
        rep = lambda a: jnp.broadcast_to(a.astype(F32)[:, :, None], (nb, H_MLSTM, HEAD_DIM))
        h_m, c_s, n_s, m_s = _mlstm_step(lq, lk, lv, og, rep(gates[:, :H_MLSTM]), rep(gates[:, H_MLSTM:2 * H_MLSTM]),
                                         rep(state_mlstm_m[l]), state_mlstm_n[l].astype(F32),
                                         state_mlstm_C[l].astype(F32), g_mo)
        mem_o = _memory_attend_decode(mq, cache_mem_k.reshape(depth, nb, n_mem * H_MEM, HEAD_DIM),
                                      cache_mem_v.reshape(depth, nb, n_mem * H_MEM, HEAD_DIM), l)
        xs = _out_project(xs, diff_o, h_m.reshape(nb, D_MLSTM), mem_o, *_split_w_out(w_out[l], F32))
        xs = _ffn_half(xs, *ffn2)
        outs[7].append(dk.reshape(nb, 1, H_DIFF, 2, DH_HALF))
        outs[8].append(dv.reshape(nb, 1, H_DIFF, HEAD_DIM))
        outs[9].append(c_s)
        outs[10].append(n_s)
        outs[11].append(m_s[:, :, 0])

    stacked = [jnp.stack(o, 0) for o in outs]
    return (xp.reshape(1, s_len, d_model), xs.reshape(nb, 1, d_model), *stacked)
```

```python
import functools
import math

import jax
import jax.numpy as jnp
from jax import lax
from jax.experimental import pallas as pl
from jax.experimental.pallas import tpu as pltpu

F32 = jnp.float32
BF16 = jnp.bfloat16

HEAD_DIM = 128
H_DIFF = 8
DH_HALF = HEAD_DIM // 2
H_MLSTM = 4
H_MEM = 4
D_DIFF = H_DIFF * HEAD_DIM
D_MLSTM = H_MLSTM * HEAD_DIM
D_MEMH = H_MEM * HEAD_DIM
ROPE_THETA = 10000.0
EPS = 1e-6
NEG = -1e30
MLSTM_CHUNK = 128

LANES = 128
GATE_LANES = LANES
MIB = 1024 * 1024

ROW_TILE = 512
FF_TILE = 512
ATTN_TILE = 1024
DECODE_PAGES = 16
SUM_ROWS = 16

VMEM_SMALL = 32
VMEM_LARGE = 48
VMEM_PROJ = 56

_NT = (((1,), (1,)), ((), ()))


def _params(semantics, vmem_mib):
    return pltpu.CompilerParams(dimension_semantics=semantics, vmem_limit_bytes=vmem_mib * MIB)


def _row_tile(m, pref):
    return pref if m % pref == 0 else m


def _rms(x, g):
    return x * lax.rsqrt(jnp.mean(x * x, axis=-1, keepdims=True) + EPS) * g


def _split(x):
    hi = x.astype(BF16)
    return hi, (x - hi.astype(F32)).astype(BF16)


def _mm(x, w):
    if w.dtype == BF16:
        return jnp.dot(x.astype(BF16), w, preferred_element_type=F32)
    wh, wl = _split(w)
    if x.dtype == BF16:
        return jnp.dot(x, wl, preferred_element_type=F32) + jnp.dot(x, wh, preferred_element_type=F32)
    xh, xl = _split(x)
    m = x.shape[0]
    top = jnp.dot(jnp.concatenate([xh, xl], axis=0), wh, preferred_element_type=F32)
    return (top[m:] + jnp.dot(xh, wl, preferred_element_type=F32)) + top[:m]


def _resident(shape):
    return pl.BlockSpec(shape, lambda *_: (0,) * len(shape), pipeline_mode=pl.Buffered(1))


def _log_sigmoid(x):
    return jnp.minimum(x, 0.0) - jnp.log1p(jnp.exp(-jnp.abs(x)))


def _ffn_kernel(*refs, n_ff, next_norm):
    if next_norm:
        x_ref, g_ref, wg_ref, wu_ref, wd_ref, gn_ref, y_ref, hn_ref, h_scr = refs
    else:
        x_ref, g_ref, wg_ref, wu_ref, wd_ref, y_ref, h_scr = refs
    j = pl.program_id(1)

    @pl.when(j == 0)
    def _():
        h_scr[...] = _rms(x_ref[...], g_ref[...]).astype(h_scr.dtype)
        y_ref[...] = jnp.zeros(y_ref.shape, F32)

    h = h_scr[...]
    gate = _mm(h, wg_ref[...])
    up = _mm(h, wu_ref[...])
    y_ref[...] += _mm(gate * jax.nn.sigmoid(gate) * up, wd_ref[...])

    @pl.when(j == n_ff - 1)
    def _():
        y = x_ref[...] + 0.5 * y_ref[...]
        y_ref[...] = y
        if next_norm:
            hn_ref[...] = _rms(y, gn_ref[...]).astype(hn_ref.dtype)


def _ffn_half(x, g, wg, wu, wd, g_next=None):
    m, d = x.shape
    d_ff = wg.shape[1]
    tm = _row_tile(m, ROW_TILE)
    tf = _row_tile(d_ff, FF_TILE)
    n_ff = d_ff // tf
    next_norm = g_next is not None
    row = pl.BlockSpec((tm, d), lambda i, j: (i, 0))
    vec = pl.BlockSpec((1, d), lambda i, j: (0, 0))
    in_specs = [row, vec, pl.BlockSpec((d, tf), lambda i, j: (0, j)), pl.BlockSpec((d, tf), lambda i, j: (0, j)),
                pl.BlockSpec((tf, d), lambda i, j: (j, 0))]
    args = [x, g.reshape(1, d), wg, wu, wd]
    out_shape = [jax.ShapeDtypeStruct((m, d), F32)]
    out_specs = [row]
    if next_norm:
        in_specs.append(vec)
        args.append(g_next.reshape(1, d))
        out_shape.append(jax.ShapeDtypeStruct((m, d), wg.dtype))
        out_specs.append(row)
    out = pl.pallas_call(
        functools.partial(_ffn_kernel, n_ff=n_ff, next_norm=next_norm),
        grid=(m // tm, n_ff),
        in_specs=in_specs,
        out_specs=out_specs,
        out_shape=out_shape,
        scratch_shapes=[pltpu.VMEM((tm, d), wg.dtype)],
        compiler_params=_params(("parallel", "arbitrary"), VMEM_LARGE),
        name="ffn_half",
    )(*args)
    return out if next_norm else out[0]


def _diffproj_kernel(*refs, q_scale, prompt):
    if prompt:
        h_ref, wq_ref, wk_ref, wv_ref, gq_ref, gk_ref, cos_ref, sin_ref, seg_ref, wvt_ref, qb_ref, kf_ref, vf_ref, kb_ref, vt_ref = refs
    else:
        h_ref, wq_ref, wk_ref, wv_ref, gq_ref, gk_ref, cos_ref, sin_ref, seg_ref, qb_ref, kf_ref, vf_ref = refs
    h = h_ref[...]
    tm = h.shape[0]
    heads = qb_ref.shape[1] // LANES
    cos = cos_ref[...]
    sin = sin_ref[...]
    seg = seg_ref[...]
    lane = lax.broadcasted_iota(jnp.int32, (tm, LANES), 1)
    first_half = (lane & (DH_HALF - 1)) < (DH_HALF // 2)

    def norm_rope(z, g):
        ss = _mm(z * z, seg)
        y = z * lax.rsqrt(ss * (1.0 / DH_HALF) + EPS) * g
        partner = jnp.where(first_half, pltpu.roll(y, LANES - DH_HALF // 2, axis=1), pltpu.roll(y, DH_HALF // 2, axis=1))
        return y * cos + partner * sin

    zq = _mm(h, wq_ref[...])
    gq = gq_ref[...]
    for t in range(heads):
        sl = slice(t * LANES, (t + 1) * LANES)
        qb_ref[:, sl] = (norm_rope(zq[:, sl], gq) * q_scale).astype(BF16)
    zk = _mm(h, wk_ref[...])
    gk = gk_ref[...]
    for t in range(heads):
        sl = slice(t * LANES, (t + 1) * LANES)
        kt = norm_rope(zk[:, sl], gk)
        kf_ref[:, sl] = kt
        if prompt:
            kb_ref[:, sl] = kt.astype(BF16)
    vf_ref[...] = _mm(h, wv_ref[...])
    if prompt:
        vt_ref[0] = lax.dot_general(wvt_ref[...], h, _NT, preferred_element_type=F32).astype(BF16)


def _diff_project(h, wq, wk, wv, g_q, g_k, cos, sin, q_scale, wv_t=None):
    m, d = h.shape
    tm = _row_tile(m, ROW_TILE)
    prompt = wv_t is not None
    tn = D_DIFF if wq.dtype == BF16 else 2 * LANES
    seg = jnp.kron(jnp.eye(2, dtype=F32), jnp.ones((DH_HALF, DH_HALF), F32)).astype(wq.dtype)
    row_in = pl.BlockSpec((tm, d), lambda i, j: (i, 0))
    w_spec = _resident((d, tn)) if tn == D_DIFF else pl.BlockSpec((d, tn), lambda i, j: (0, j))
    vec = pl.BlockSpec((1, LANES), lambda i, j: (0, 0))
    tab = pl.BlockSpec((tm, LANES), lambda i, j: (i, 0))
    row_out = pl.BlockSpec((tm, tn), lambda i, j: (i, j))
    in_specs = [row_in, w_spec, w_spec, w_spec, vec, vec, tab, tab, pl.BlockSpec((LANES, LANES), lambda i, j: (0, 0))]
    args = [h, wq, wk, wv, jnp.tile(g_q, 2).reshape(1, LANES), jnp.tile(g_k, 2).reshape(1, LANES), cos, sin, seg]
    out_specs = [row_out] * 3
    out_shape = [jax.ShapeDtypeStruct((m, D_DIFF), dt) for dt in (BF16, F32, F32)]
    if prompt:
        assert tn == D_DIFF
        in_specs.append(_resident((D_DIFF, d)))
        args.append(wv_t)
        out_specs += [row_out, pl.BlockSpec((1, D_DIFF, tm), lambda i, j: (i, 0, 0))]
        out_shape += [jax.ShapeDtypeStruct((m, D_DIFF), BF16), jax.ShapeDtypeStruct((m // tm, D_DIFF, tm), BF16)]
    return pl.pallas_call(
        functools.partial(_diffproj_kernel, q_scale=q_scale, prompt=prompt),
        grid=(m // tm, D_DIFF // tn),
        in_specs=in_specs,
        out_specs=out_specs,
        out_shape=out_shape,
        compiler_params=_params(("parallel", "arbitrary"), VMEM_PROJ),
        name="diff_project",
    )(*args)


def _mlstmproj_kernel(h_ref, wq_ref, wk_ref, wv_ref, wo_ref, wg_ref, b_ref, q_ref, k_ref, v_ref, og_ref, gt_ref, *,
                      transpose_gates):
    h = h_ref[...]
    tm = h.shape[0]
    q_ref[...] = _mm(h, wq_ref[...]).astype(q_ref.dtype)
    k_ref[...] = (_mm(h, wk_ref[...]) * (HEAD_DIM ** -0.5)).astype(k_ref.dtype)
    v_ref[...] = _mm(h, wv_ref[...]).astype(v_ref.dtype)
    og_ref[...] = jax.nn.sigmoid(_mm(h, wo_ref[...]))
    zg = _mm(h, wg_ref[...]) + b_ref[...]
    lane = lax.broadcasted_iota(jnp.int32, (tm, GATE_LANES), 1)
    gates = jnp.where(lane < H_MLSTM, zg, _log_sigmoid(zg))
    if transpose_gates:
        for c in range(tm // MLSTM_CHUNK):
            gt = jnp.transpose(gates[c * MLSTM_CHUNK:(c + 1) * MLSTM_CHUNK, :])
            gt_ref[c] = gt[:2 * H_MLSTM, :]
    else:
        gt_ref[...] = gates


def _mlstm_project(h, w, w_gate, bias, transpose_gates):
    m, d = h.shape
    dt = w[0].dtype
    tm = _row_tile(m, ROW_TILE)
    row_in = pl.BlockSpec((tm, d), lambda i: (i, 0))
    row_out = pl.BlockSpec((tm, D_MLSTM), lambda i: (i, 0))
    if transpose_gates:
        cpt = tm // MLSTM_CHUNK
        g_shape = jax.ShapeDtypeStruct((m // MLSTM_CHUNK, 2 * H_MLSTM, MLSTM_CHUNK), F32)
        g_spec = pl.BlockSpec((cpt, 2 * H_MLSTM, MLSTM_CHUNK), lambda i: (i, 0, 0))
    else:
        g_shape = jax.ShapeDtypeStruct((m, GATE_LANES), F32)
        g_spec = pl.BlockSpec((tm, GATE_LANES), lambda i: (i, 0))
    return pl.pallas_call(
        functools.partial(_mlstmproj_kernel, transpose_gates=transpose_gates),
        grid=(m // tm,),
        in_specs=[row_in] + [_resident((d, D_MLSTM))] * 4 + [_resident((d, GATE_LANES)), _resident((1, GATE_LANES))],
        out_specs=[row_out, row_out, row_out, row_out, g_spec],
        out_shape=[jax.ShapeDtypeStruct((m, D_MLSTM), t) for t in (dt, dt, dt, F32)] + [g_shape],
        compiler_params=_params(("parallel",), VMEM_LARGE),
        name="mlstm_project",
    )(h, *w, w_gate, bias)


def _softmax_pv(s, v):
    m = jnp.max(s, axis=-1, keepdims=True)
    p = jnp.exp(s - m)
    l = jnp.sum(p, axis=-1, keepdims=True)
    return jnp.dot(p.astype(BF16), v, preferred_element_type=F32) / l


def _memq_kernel(*refs, attend):
    if attend:
        h_ref, w_ref, gq_ref, mk_ref, mv_ref, o_ref = refs
    else:
        h_ref, w_ref, gq_ref, o_ref = refs
    z = _mm(h_ref[...], w_ref[...])
    gq = gq_ref[...]
    for t in range(H_MEM):
        sl = slice(t * HEAD_DIM, (t + 1) * HEAD_DIM)
        q = _rms(z[:, sl], gq)
        if attend:
            s = lax.dot_general(q.astype(BF16), mk_ref[:, sl], _NT, preferred_element_type=F32) * (HEAD_DIM ** -0.5)
            q = _softmax_pv(s, mv_ref[:, sl])
        o_ref[:, sl] = q.astype(BF16)


def _mem_query(h, w, g_q, mem_k=None, mem_v=None):
    m, d = h.shape
    tm = _row_tile(m, ROW_TILE)
    attend = mem_k is not None
    in_specs = [pl.BlockSpec((tm, d), lambda i: (i, 0)), pl.BlockSpec((d, D_MEMH), lambda i: (0, 0)),
                pl.BlockSpec((1, HEAD_DIM), lambda i: (0, 0))]
    args = [h, w, g_q.reshape(1, HEAD_DIM)]
    if attend:
        n_mem = mem_k.shape[0]
        in_specs += [pl.BlockSpec((n_mem, D_MEMH), lambda i: (0, 0))] * 2
        args += [mem_k, mem_v]
    return pl.pallas_call(
        functools.partial(_memq_kernel, attend=attend),
        grid=(m // tm,),
        in_specs=in_specs,
        out_specs=pl.BlockSpec((tm, D_MEMH), lambda i: (i, 0)),
        out_shape=jax.ShapeDtypeStruct((m, D_MEMH), BF16),
        compiler_params=_params(("parallel",), VMEM_SMALL),
        name="mem_query",
    )(*args)


def _memkv_kernel(mem_ref, g_ref, wk_ref, wv_ref, gk_ref, kf_ref, vf_ref, kb_ref, vb_ref):
    mn = _rms(mem_ref[...], g_ref[...]).astype(BF16)
    zk = jnp.dot(mn, wk_ref[...], preferred_element_type=F32)
    gk = gk_ref[...]
    for t in range(H_MEM):
        sl = slice(t * HEAD_DIM, (t + 1) * HEAD_DIM)
        kt = _rms(zk[:, sl], gk)
        kf_ref[:, sl] = kt
        kb_ref[:, sl] = kt.astype(BF16)
    zv = jnp.dot(mn, wv_ref[...], preferred_element_type=F32)
    vf_ref[...] = zv
    vb_ref[...] = zv.astype(BF16)


def _memory_kv(mem, g_in, wk, wv, g_k):
    n_mem, d = mem.shape
    shapes = [jax.ShapeDtypeStruct((n_mem, D_MEMH), dt) for dt in (F32, F32, BF16, BF16)]
    return pl.pallas_call(
        _memkv_kernel,
        out_shape=shapes,
        compiler_params=pltpu.CompilerParams(vmem_limit_bytes=VMEM_SMALL * MIB),
        name="memory_kv",
    )(mem, g_in.reshape(1, d), wk, wv, g_k.reshape(1, HEAD_DIM))


def _lambda_kernel(q1_ref, k1_ref, q2_ref, k2_ref, o_ref, *, lambda_init):
    a = jnp.exp(jnp.sum(q1_ref[...] * k1_ref[...], axis=-1, keepdims=True))
    b = jnp.exp(jnp.sum(q2_ref[...] * k2_ref[...], axis=-1, keepdims=True))
    o_ref[...] = a - b + lambda_init


def _diff_lambda(q1, k1, q2, k2, lambda_init):
    r = lambda a: a.reshape(1, DH_HALF).astype(F32)
    return pl.pallas_call(
        functools.partial(_lambda_kernel, lambda_init=lambda_init),
        out_shape=jax.ShapeDtypeStruct((1, 1), F32),
        name="diff_lambda",
    )(r(q1), r(k1), r(q2), r(k2))


def _split_maps(q):
    lane = lax.broadcasted_iota(jnp.int32, q.shape, 1)
    qf = q.astype(F32)
    return jnp.concatenate([jnp.where(lane < DH_HALF, qf, 0.0), jnp.where(lane >= DH_HALF, qf, 0.0)], axis=0).astype(BF16)


def _subln(d, g, out_scale):
    return _rms(d, g) * out_scale


def _attn_kernel(lam_ref, q_ref, k_ref, vt_ref, gs_ref, o_ref, m_scr, acc_scr, *, tq, out_scale):
    i = pl.program_id(1)
    qs = _split_maps(q_ref[...])
    m_scr[...] = jnp.full(m_scr.shape, NEG, F32)
    acc_scr[...] = jnp.zeros(acc_scr.shape, F32)
    tv = vt_ref.shape[2]
    ones_rows = (lax.broadcasted_iota(jnp.int32, (SUM_ROWS, tv), 0) == 0).astype(F32).astype(BF16)

    def v_tile(idx):
        return jnp.concatenate([vt_ref[idx], ones_rows], axis=0)

    def step(j, masked):
        kj = k_ref[pl.ds(pl.multiple_of(j * tq, tq), tq), :]
        s = lax.dot_general(kj, qs, _NT, preferred_element_type=F32)
        if masked:
            key = lax.broadcasted_iota(jnp.int32, s.shape, 0)
            qry = lax.broadcasted_iota(jnp.int32, s.shape, 1)
            s = jnp.where(key <= jnp.where(qry >= tq, qry - tq, qry), s, NEG)
        m_old = m_scr[...]
        m_new = jnp.maximum(m_old, jnp.max(s, axis=0, keepdims=True))
        alpha = jnp.exp2(m_old - m_new)
        pb = jnp.exp2(s - m_new).astype(BF16)
        pv = jnp.dot(v_tile(j * (tq // tv)), pb[:tv], preferred_element_type=F32)
        for u in range(1, tq // tv):
            pv += jnp.dot(v_tile(j * (tq // tv) + u), pb[u * tv:(u + 1) * tv], preferred_element_type=F32)
        acc_scr[...] = alpha * acc_scr[...] + pv
        m_scr[...] = m_new

    def body(j, carry):
        step(j, False)
        return carry

    lax.fori_loop(0, i, body, 0)
    step(i, True)

    o_t = acc_scr[:HEAD_DIM, :] / acc_scr[HEAD_DIM:HEAD_DIM + 1, :]
    d = jnp.transpose(o_t[:, :tq] - lam_ref[0, 0] * o_t[:, tq:])
    o_ref[...] = _subln(d, gs_ref[...], out_scale).astype(BF16)


def _diff_attention_prompt(lam, q, k, v_t, g_subln, out_scale):
    s_len = q.shape[0]
    tv = v_t.shape[2]
    tq = _row_tile(s_len, ATTN_TILE)
    assert tq % tv == 0
    blk = pl.BlockSpec((tq, HEAD_DIM), lambda h, i: (i, h))
    return pl.pallas_call(
        functools.partial(_attn_kernel, tq=tq, out_scale=out_scale),
        grid=(H_DIFF, s_len // tq),
        in_specs=[pl.BlockSpec(memory_space=pltpu.SMEM), blk, pl.BlockSpec((s_len, HEAD_DIM), lambda h, i: (0, h)),
                  pl.BlockSpec((s_len // tv, HEAD_DIM, tv), lambda h, i: (0, h, 0)),
                  pl.BlockSpec((1, HEAD_DIM), lambda h, i: (0, 0))],
        out_specs=blk,
        out_shape=jax.ShapeDtypeStruct((s_len, D_DIFF), BF16),
        scratch_shapes=[pltpu.VMEM((1, 2 * tq), F32), pltpu.VMEM((HEAD_DIM + SUM_ROWS, 2 * tq), F32)],
        compiler_params=_params(("parallel", "arbitrary"), VMEM_LARGE),
        name="diff_attention_prompt",
    )(lam, q, k, v_t, g_subln.reshape(1, HEAD_DIM))


def _decode_attn_kernel(pt_ref, lam_ref, q_ref, kn_ref, vn_ref, gs_ref, *refs, pages, out_scale):
    k_refs = refs[:pages]
    v_refs = refs[pages:2 * pages]
    o_ref, qs_scr, m_scr, l_scr, acc_scr = refs[2 * pages:]
    p_idx = pl.program_id(1)
    n_maps = 2 * H_DIFF
    page = k_refs[0].shape[1]

    @pl.when(p_idx == 0)
    def _():
        row = lax.broadcasted_iota(jnp.int32, (n_maps, D_DIFF), 0)
        col = lax.broadcasted_iota(jnp.int32, (n_maps, D_DIFF), 1)
        qb = jnp.broadcast_to(q_ref[0].astype(F32), (n_maps, D_DIFF))
        qs_scr[...] = jnp.where(col // DH_HALF == row, qb, 0.0).astype(BF16)
        m_scr[...] = jnp.full(m_scr.shape, NEG, F32)
        l_scr[...] = jnp.zeros(l_scr.shape, F32)
        acc_scr[...] = jnp.zeros(acc_scr.shape, F32)

    qs = qs_scr[...]
    s = jnp.concatenate(
        [lax.dot_general(qs, k_refs[t][0], _NT, preferred_element_type=F32) for t in range(pages)], axis=1)
    m_old = m_scr[...]
    m_new = jnp.maximum(m_old, jnp.max(s, axis=-1, keepdims=True))
    alpha = jnp.exp(m_old - m_new)
    p = jnp.exp(s - m_new).astype(BF16)
    l_scr[...] = alpha * l_scr[...] + jnp.sum(p.astype(F32), axis=-1, keepdims=True)
    for h in range(H_DIFF):
        sl = slice(h * HEAD_DIM, (h + 1) * HEAD_DIM)
        rows = pl.ds(h, page, stride=H_DIFF)
        pv = jnp.dot(p[:, :page], v_refs[0][0, 0, rows, :].astype(BF16), preferred_element_type=F32)
        for t in range(1, pages):
            pv += jnp.dot(p[:, t * page:(t + 1) * page], v_refs[t][0, 0, rows, :].astype(BF16),
                          preferred_element_type=F32)
        acc_scr[:, sl] = alpha * acc_scr[:, sl] + pv
    m_scr[...] = m_new

    @pl.when(p_idx == pl.num_programs(1) - 1)
    def _():
        s_new = jnp.sum(qs.astype(F32) * kn_ref[0], axis=-1, keepdims=True)
        m_fin = jnp.maximum(m_new, s_new)
        a_fin = jnp.exp(m_new - m_fin)
        p_new = jnp.exp(s_new - m_fin)
        l_fin = a_fin * l_scr[...] + p_new
        o = (a_fin * acc_scr[...] + p_new * vn_ref[0]) / l_fin
        lam = lam_ref[0, 0]
        gs = gs_ref[...]
        for h in range(H_DIFF):
            sl = slice(h * HEAD_DIM, (h + 1) * HEAD_DIM)
            d = o[2 * h:2 * h + 1, sl] - lam * o[2 * h + 1:2 * h + 2, sl]
            o_ref[0, :, sl] = _subln(d, gs, out_scale).astype(BF16)


def _diff_attention_decode(lam, q, k_new, v_new, cache_k, cache_v, layer, page_table, g_subln, out_scale,
                           pages_per_step=DECODE_PAGES):
    nb = q.shape[0]
    n_pages = page_table.shape[1]
    n_maps = 2 * H_DIFF
    page = cache_k.shape[1]
    pages = pages_per_step if n_pages % pages_per_step == 0 else 1
    tok = pl.BlockSpec((1, 1, D_DIFF), lambda b, p, pt: (b, 0, 0))

    def k_spec(t):
        return pl.BlockSpec((1, page, D_DIFF), lambda b, p, pt: (pt[b, p * pages + t], 0, 0))

    def v_spec(t):
        return pl.BlockSpec((1, 1, page * H_DIFF, HEAD_DIM), lambda b, p, pt: (layer, pt[b, p * pages + t], 0, 0))

    grid_spec = pltpu.PrefetchScalarGridSpec(
        num_scalar_prefetch=1,
        grid=(nb, n_pages // pages),
        in_specs=[pl.BlockSpec(memory_space=pltpu.SMEM), tok, tok, tok, pl.BlockSpec((1, HEAD_DIM), lambda b, p, pt: (0, 0))]
        + [k_spec(t) for t in range(pages)] + [v_spec(t) for t in range(pages)],
        out_specs=tok,
        scratch_shapes=[pltpu.VMEM((n_maps, D_DIFF), BF16), pltpu.VMEM((n_maps, 1), F32), pltpu.VMEM((n_maps, 1), F32),
                        pltpu.VMEM((n_maps, D_DIFF), F32)],
    )
    out = pl.pallas_call(
        functools.partial(_decode_attn_kernel, pages=pages, out_scale=out_scale),
        grid_spec=grid_spec,
        out_shape=jax.ShapeDtypeStruct((nb, 1, D_DIFF), BF16),
        compiler_params=_params(("parallel", "arbitrary"), VMEM_LARGE),
        name="diff_attention_decode",
    )(page_table, lam, q.reshape(nb, 1, D_DIFF), k_new.reshape(nb, 1, D_DIFF), v_new.reshape(nb, 1, D_DIFF),
      g_subln.reshape(1, HEAD_DIM), *([cache_k] * pages), *([cache_v] * pages))
    return out.reshape(nb, D_DIFF)


def _mlstm_kernel(q_ref, k_ref, v_ref, og_ref, gt_ref, go_ref, hm_ref, c_ref, n_ref, m_ref):
    @pl.when(pl.program_id(0) == 0)
    def _():
        c_ref[...] = jnp.zeros(c_ref.shape, F32)
        n_ref[...] = jnp.zeros(n_ref.shape, F32)
        m_ref[...] = jnp.zeros(m_ref.shape, F32)

    L = MLSTM_CHUNK
    lane8 = lax.broadcasted_iota(jnp.int32, (2 * H_MLSTM, L), 1)
    row = lax.broadcasted_iota(jnp.int32, (L, L), 0)
    col = lax.broadcasted_iota(jnp.int32, (L, L), 1)
    causal = col <= row

    def chunk(c, carry):
        r0 = pl.multiple_of(c * L, L)
        g8 = gt_ref[c]
        cs = g8
        d = 1
        while d < L:
            cs = cs + jnp.where(lane8 >= d, pltpu.roll(cs, d, axis=1), 0.0)
            d *= 2
        stacked = jnp.concatenate([g8, cs, jnp.zeros((L - 4 * H_MLSTM, L), F32)], axis=0)
        cols = jnp.transpose(stacked)
        for h in range(H_MLSTM):
            sl = slice(h * HEAD_DIM, (h + 1) * HEAD_DIM)
            q = q_ref[pl.ds(r0, L), sl]
            k = k_ref[pl.ds(r0, L), sl]
            v = v_ref[pl.ds(r0, L), sl]
            ig_r = g8[h:h + 1, :]
            bt_r = cs[H_MLSTM + h:H_MLSTM + h + 1, :]
            ig_c = cols[:, h:h + 1]
            bt_c = cols[:, 3 * H_MLSTM + h:3 * H_MLSTM + h + 1]
            m0 = m_ref[h:h + 1, 0:1]
            c0 = c_ref[h]
            n0 = n_ref[h:h + 1, :]

            dm = jnp.where(causal, bt_c + (ig_r - bt_r), NEG)
            inter = bt_c + m0
            m_c = jnp.maximum(inter, jnp.max(dm, axis=-1, keepdims=True))
            w_intra = jnp.exp(dm - m_c)
            w_inter = jnp.exp(inter - m_c)
            a = w_intra * lax.dot_general(q, k, _NT, preferred_element_type=F32)
            cq = lax.dot_general(q, c0.astype(BF16), _NT, preferred_element_type=F32)
            num = jnp.dot(a.astype(BF16), v, preferred_element_type=F32) + w_inter * cq
            nq = jnp.sum(q.astype(F32) * n0, axis=-1, keepdims=True)
            den = jnp.sum(a, axis=-1, keepdims=True) + w_inter * nq
            hh = num / jnp.maximum(jnp.abs(den), jnp.exp(-m_c))
            hm_ref[pl.ds(r0, L), sl] = (_rms(hh, go_ref[:, sl]) * og_ref[pl.ds(r0, L), sl]).astype(BF16)

            bl = bt_r[:, L - 1:L]
            ml = m_c[L - 1:L, :]
            w_end = jnp.exp(bl - bt_c + ig_c - ml)
            decay = jnp.exp(bl + m0 - ml)
            vw_t = jnp.transpose(w_end * v.astype(F32)).astype(BF16)
            c_ref[h] = decay * c0 + jnp.dot(vw_t, k, preferred_element_type=F32)
            n_ref[h:h + 1, :] = decay * n0 + jnp.sum(w_end * k.astype(F32), axis=0, keepdims=True)
            m_ref[h:h + 1, :] = jnp.broadcast_to(ml, (1, LANES))
        return carry

    lax.fori_loop(0, gt_ref.shape[0], chunk, 0)


def _mlstm_prompt(q, k, v, og, gates_t, g_out):
    s_len = q.shape[0]
    tm = _row_tile(s_len, ROW_TILE)
    cpt = tm // MLSTM_CHUNK
    row = pl.BlockSpec((tm, D_MLSTM), lambda t: (t, 0))
    return pl.pallas_call(
        _mlstm_kernel,
        grid=(s_len // tm,),
        in_specs=[row, row, row, row, pl.BlockSpec((cpt, 2 * H_MLSTM, MLSTM_CHUNK), lambda t: (t, 0, 0)),
                  pl.BlockSpec((1, D_MLSTM), lambda t: (0, 0))],
        out_specs=[row, pl.BlockSpec((H_MLSTM, HEAD_DIM, HEAD_DIM), lambda t: (0, 0, 0)),
                   pl.BlockSpec((H_MLSTM, HEAD_DIM), lambda t: (0, 0)), pl.BlockSpec((H_MLSTM, LANES), lambda t: (0, 0))],
        out_shape=[jax.ShapeDtypeStruct((s_len, D_MLSTM), BF16), jax.ShapeDtypeStruct((H_MLSTM, HEAD_DIM, HEAD_DIM), F32),
                   jax.ShapeDtypeStruct((H_MLSTM, HEAD_DIM), F32), jax.ShapeDtypeStruct((H_MLSTM, LANES), F32)],
        compiler_params=_params(("arbitrary",), VMEM_SMALL),
        name="mlstm_prompt",
    )(q, k, v, og, gates_t, g_out.reshape(1, D_MLSTM))


def _mlstm_step_kernel(q_ref, k_ref, v_ref, og_ref, ig_ref, lf_ref, m0_ref, n0_ref, c0_ref, go_ref,
                       hm_ref, c1_ref, n1_ref, m1_ref):
    q = q_ref[0].astype(F32)
    k = k_ref[0].astype(F32)
    v = v_ref[0].astype(F32)
    ig = ig_ref[0]
    lf = lf_ref[0]
    m0 = m0_ref[0]
    n0 = n0_ref[0]
    m1 = jnp.maximum(lf + m0, ig)
    w_i = jnp.exp(ig - m1)
    w_f = jnp.exp(lf + m0 - m1)
    a = w_i * jnp.sum(q * k, axis=-1, keepdims=True)
    cq_rows = []
    for h in range(H_MLSTM):
        qh = jnp.broadcast_to(q[h:h + 1, :], (8, HEAD_DIM)).astype(BF16)
        cq = lax.dot_general(qh, c0_ref[0, h].astype(BF16), _NT, preferred_element_type=F32)
        cq_rows.append(cq[0:1, :])
    cq = jnp.concatenate(cq_rows, axis=0)
    num = a * v + w_f * cq
    den = a + w_f * jnp.sum(n0 * q, axis=-1, keepdims=True)
    hh = num / jnp.maximum(jnp.abs(den), jnp.exp(-m1))
    hm_ref[0] = (_rms(hh, go_ref[...]) * og_ref[0]).astype(BF16)
    n1_ref[0] = w_f * n0 + w_i * k
    m1_ref[0] = m1
    wv = w_i * v
    wv_cols = jnp.transpose(jnp.concatenate([wv, jnp.zeros((HEAD_DIM - H_MLSTM, HEAD_DIM), F32)], axis=0))
    for h in range(H_MLSTM):
        c1_ref[0, h] = w_f[h:h + 1, 0:1] * c0_ref[0, h] + wv_cols[:, h:h + 1] * k[h:h + 1, :]


def _mlstm_step(q, k, v, og, ig, lf, m0, n0, c0, g_out):
    nb = q.shape[0]
    hd = (nb, H_MLSTM, HEAD_DIM)
    tok = pl.BlockSpec((1, H_MLSTM, HEAD_DIM), lambda b: (b, 0, 0))
    mat = pl.BlockSpec((1, H_MLSTM, HEAD_DIM, HEAD_DIM), lambda b: (b, 0, 0, 0))
    return pl.pallas_call(
        _mlstm_step_kernel,
        grid=(nb,),
        in_specs=[tok] * 8 + [mat, pl.BlockSpec((H_MLSTM, HEAD_DIM), lambda b: (0, 0))],
        out_specs=[tok, mat, tok, tok],
        out_shape=[jax.ShapeDtypeStruct(hd, BF16), jax.ShapeDtypeStruct((nb, H_MLSTM, HEAD_DIM, HEAD_DIM), F32),
                   jax.ShapeDtypeStruct(hd, F32), jax.ShapeDtypeStruct(hd, F32)],
        compiler_params=_params(("parallel",), VMEM_SMALL),
        name="mlstm_step",
    )(q.reshape(hd), k.reshape(hd), v.reshape(hd), og.reshape(hd), ig, lf, m0, n0, c0, g_out)


def _mem_decode_kernel(q_ref, k_ref, v_ref, o_ref):
    n_mem = k_ref.shape[2] // H_MEM
    for t in range(H_MEM):
        sl = slice(t * HEAD_DIM, (t + 1) * HEAD_DIM)
        rows = pl.ds(t, n_mem, stride=H_MEM)
        q = jnp.broadcast_to(q_ref[0, :, sl], (8, HEAD_DIM))
        s = lax.dot_general(q, k_ref[0, 0, rows, :].astype(BF16), _NT, preferred_element_type=F32) * (HEAD_DIM ** -0.5)
        o = _softmax_pv(s, v_ref[0, 0, rows, :].astype(BF16))
        o_ref[0, :, sl] = o[0:1, :].astype(BF16)


def _memory_attend_decode(q, mem_k, mem_v, layer):
    _, nb, rows, _ = mem_k.shape
    tok = pl.BlockSpec((1, 1, D_MEMH), lambda b: (b, 0, 0))
    mem = pl.BlockSpec((1, 1, rows, HEAD_DIM), lambda b: (layer, b, 0, 0))
    out = pl.pallas_call(
        _mem_decode_kernel,
        grid=(nb,),
        in_specs=[tok, mem, mem],
        out_specs=tok,
        out_shape=jax.ShapeDtypeStruct((nb, 1, D_MEMH), BF16),
        compiler_params=_params(("parallel",), VMEM_SMALL),
        name="memory_attend_decode",
    )(q.reshape(nb, 1, D_MEMH), mem_k, mem_v)
    return out.reshape(nb, D_MEMH)


def _outproj_kernel(x_ref, d_ref, m_ref, c_ref, wd_ref, wm_ref, wc_ref, y_ref):
    y = x_ref[...] + _mm(d_ref[...], wd_ref[...])
    y += _mm(m_ref[...], wm_ref[...])
    y += _mm(c_ref[...], wc_ref[...])
    y_ref[...] = y


def _out_project(x, d, m, c, w_d, w_m, w_c):
    rows, dm = x.shape
    tm = _row_tile(rows, ROW_TILE)

    def row(width):
        return pl.BlockSpec((tm, width), lambda i: (i, 0))

    return pl.pallas_call(
        _outproj_kernel,
        grid=(rows // tm,),
        in_specs=[row(dm), row(D_DIFF), row(D_MLSTM), row(D_MEMH), _resident(w_d.shape), _resident(w_m.shape),
                  _resident(w_c.shape)],
        out_specs=row(dm),
        out_shape=jax.ShapeDtypeStruct((rows, dm), F32),
        compiler_params=_params(("parallel",), VMEM_LARGE),
        name="out_project",
    )(x, d, m, c, w_d, w_m, w_c)


def _rope_tables(pos):
    inv = 1.0 / (ROPE_THETA ** (jnp.arange(0, DH_HALF, 2, dtype=F32) / DH_HALF))
    half = DH_HALF // 2
    ang = pos.astype(F32)[:, None] * jnp.tile(inv, LANES // half)[None, :]
    sign = jnp.where((jnp.arange(LANES) // half) % 2 == 0, -1.0, 1.0).astype(F32)
    return jnp.cos(ang), jnp.sin(ang) * sign[None, :]


def _split_w_in(w_in, dtype):
    o = 0
    wq, wk, wv = (w_in[:, o + t * D_DIFF:o + (t + 1) * D_DIFF].astype(dtype) for t in range(3))
    o = 3 * D_DIFF
    w_ml = tuple(w_in[:, o + t * D_MLSTM:o + (t + 1) * D_MLSTM].astype(dtype) for t in range(4))
    o += 4 * D_MLSTM
    w_gate = jnp.pad(w_in[:, o:o + 2 * H_MLSTM], ((0, 0), (0, GATE_LANES - 2 * H_MLSTM))).astype(dtype)
    o += 2 * H_MLSTM
    w_mq = w_in[:, o:o + D_MEMH].astype(dtype)
    return wq, wk, wv, w_ml, w_gate, w_mq


def _split_w_out(w_out, dtype):
    w = w_out.astype(dtype)
    return w[:D_DIFF], w[D_DIFF:D_DIFF + D_MLSTM], w[D_DIFF + D_MLSTM:]


def kernel(x_prompt, x_sample, cache_diff_k, cache_diff_v, cache_mem_k, cache_mem_v, state_mlstm_C, state_mlstm_n,
           state_mlstm_m, page_table, mem_prompt, g_ffn1, w_ffn1_gate, w_ffn1_up, w_ffn1_down, g_mix, w_in, b_igate,
           b_fgate, g_q_diff, g_k_diff, lambda_q1, lambda_k1, lambda_q2, lambda_k2, g_subln, g_mlstm_out, g_mem_in,
           w_mem_k, w_mem_v, g_q_mem, g_k_mem, w_out, g_ffn2, w_ffn2_gate, w_ffn2_up, w_ffn2_down):
    depth = w_in.shape[0]
    bp, s_len, d_model = x_prompt.shape
    nb, t_s, _ = x_sample.shape
    assert bp == 1 and t_s == 1, "one prompt sequence and one new token per decode request"
    n_phys, page = cache_diff_k.shape[1], cache_diff_k.shape[2]
    n_past = page_table.shape[1] * page
    n_mem = mem_prompt.shape[1]

    xp = x_prompt.reshape(s_len, d_model)
    xs = x_sample.reshape(nb, d_model)
    cos_p, sin_p = _rope_tables(jnp.arange(s_len))
    cos_s, sin_s = (jnp.broadcast_to(t, (nb, LANES)) for t in _rope_tables(jnp.full((1,), n_past)))

    outs = [[] for _ in range(12)]
    for l in range(depth):
        lambda_init = 0.8 - 0.6 * math.exp(-0.3 * l)
        out_scale = 1.0 - lambda_init
        bf = lambda a: a[l].astype(BF16)
        ffn1 = (g_ffn1[l], bf(w_ffn1_gate), bf(w_ffn1_up), bf(w_ffn1_down))
        ffn2 = (g_ffn2[l], bf(w_ffn2_gate), bf(w_ffn2_up), bf(w_ffn2_down))
        wq, wk, wv, w_ml, w_gate, w_mq = _split_w_in(w_in[l], BF16)
        gate_bias = jnp.pad(jnp.concatenate([b_igate[l], b_fgate[l]]).astype(F32),
                            (0, GATE_LANES - 2 * H_MLSTM)).reshape(1, GATE_LANES)
        lam = _diff_lambda(lambda_q1[l], lambda_k1[l], lambda_q2[l], lambda_k2[l], lambda_init)
        g_mo = g_mlstm_out[l].reshape(H_MLSTM, HEAD_DIM)

        xp, hp = _ffn_half(xp, *ffn1, g_next=g_mix[l])
        dq, dk, dv, dk_b, dv_t = _diff_project(hp, wq, wk, wv, g_q_diff[l], g_k_diff[l], cos_p, sin_p,
                                               DH_HALF ** -0.5 * math.log2(math.e), wv_t=wv.T)
        lq, lk, lv, og, gates_t = _mlstm_project(hp, w_ml, w_gate, gate_bias, transpose_gates=True)
        mk, mv, mk_b, mv_b = _memory_kv(mem_prompt.reshape(n_mem, d_model), g_mem_in[l], bf(w_mem_k), bf(w_mem_v),
                                        g_k_mem[l])
        mem_o = _mem_query(hp, w_mq, g_q_mem[l], mk_b, mv_b)
        diff_o = _diff_attention_prompt(lam, dq, dk_b, dv_t, g_subln[l], out_scale)
        h_m, c_p, n_p, m_p = _mlstm_prompt(lq, lk, lv, og, gates_t, g_mo)
        xp = _out_project(xp, diff_o, h_m, mem_o, *_split_w_out(w_out[l], BF16))
        xp = _ffn_half(xp, *ffn2)
        outs[0].append(dk.reshape(1, s_len, H_DIFF, 2, DH_HALF))
        outs[1].append(dv.reshape(1, s_len, H_DIFF, HEAD_DIM))
        outs[2].append(mk.reshape(1, n_mem, H_MEM, HEAD_DIM))
        outs[3].append(mv.reshape(1, n_mem, H_MEM, HEAD_DIM))
        outs[4].append(c_p.reshape(1, H_MLSTM, HEAD_DIM, HEAD_DIM))
        outs[5].append(n_p.reshape(1, H_MLSTM, HEAD_DIM))
        outs[6].append(m_p[:, 0].reshape(1, H_MLSTM))

        ffn1 = (g_ffn1[l], w_ffn1_gate[l], w_ffn1_up[l], w_ffn1_down[l])
        ffn2 = (g_ffn2[l], w_ffn2_gate[l], w_ffn2_up[l], w_ffn2_down[l])
        wq, wk, wv, w_ml, w_gate, w_mq = _split_w_in(w_in[l], F32)
        xs, hs = _ffn_half(xs, *ffn1, g_next=g_mix[l])
        dq, dk, dv = _diff_project(hs, wq, wk, wv, g_q_diff[l], g_k_diff[l], cos_s, sin_s, DH_HALF ** -0.5)
        lq, lk, lv, og, gates = _mlstm_project(hs, w_ml, w_gate, gate_bias, transpose_gates=False)
        mq = _mem_query(hs, w_mq, g_q_mem[l])
        diff_o = _diff_attention_decode(lam, dq, dk, dv, cache_diff_k[l].astype(BF16).reshape(n_phys, page, D_DIFF),
                                        cache_diff_v.reshape(depth, n_phys, page * H_DIFF, HEAD_DIM), l, page_table,
                                        g_subln[l], out_scale)
        rep = lambda a: jnp.broadcast_to(a.astype(F32)[:, :, None], (nb, H_MLSTM, HEAD_DIM))
        h_m, c_s, n_s, m_s = _mlstm_step(lq, lk, lv, og, rep(gates[:, :H_MLSTM]), rep(gates[:, H_MLSTM:2 * H_MLSTM]),
                                         rep(state_mlstm_m[l]), state_mlstm_n[l].astype(F32),
                                         state_mlstm_C[l].astype(F32), g_mo)
        mem_o = _memory_attend_decode(mq, cache_mem_k.reshape(depth, nb, n_mem * H_MEM, HEAD_DIM),
                                      cache_mem_v.reshape(depth, nb, n_mem * H_MEM, HEAD_DIM), l)
        xs = _out_project(xs, diff_o, h_m.reshape(nb, D_MLSTM), mem_o, *_split_w_out(w_out[l], F32))
        xs = _ffn_half(xs, *ffn2)
        outs[7].append(dk.reshape(nb, 1, H_DIFF, 2, DH_HALF))
        outs[8].append(dv.reshape(nb, 1, H_DIFF, HEAD_DIM))
        outs[9].append(c_s)
        outs[10].append(n_s)
        outs[11].append(m_s[:, :, 0])

    stacked = [jnp.stack(o, 0) for o in outs]
    return (xp.reshape(1, s_len, d_model), xs.reshape(nb, 1, d_model), *stacked)
```

```python
import functools
import math

import jax
import jax.numpy as jnp
from jax import lax
from jax.experimental import pallas as pl
from jax.experimental.pallas import tpu as pltpu

F32 = jnp.float32
BF16 = jnp.bfloat16

HEAD_DIM = 128
H_DIFF = 8
DH_HALF = HEAD_DIM // 2
H_MLSTM = 4
H_MEM = 4
D_DIFF = H_DIFF * HEAD_DIM
D_MLSTM = H_MLSTM * HEAD_DIM
D_MEMH = H_MEM * HEAD_DIM
ROPE_THETA = 10000.0
EPS = 1e-6
NEG = -1e30
MLSTM_CHUNK = 128

LANES = 128
GATE_LANES = LANES
MIB = 1024 * 1024

ROW_TILE = 512
FF_TILE = 512
ATTN_TILE = 1024
DECODE_PAGES = 16
SUM_ROWS = 16

VMEM_SMALL = 32
VMEM_LARGE = 48
VMEM_PROJ = 56

_NT = (((1,), (1,)), ((), ()))


def _params(semantics, vmem_mib):
    return pltpu.CompilerParams(dimension_semantics=semantics, vmem_limit_bytes=vmem_mib * MIB)


def _row_tile(m, pref):
    return pref if m % pref == 0 else m


def _rms(x, g):
    return x * lax.rsqrt(jnp.mean(x * x, axis=-1, keepdims=True) + EPS) * g


def _split(x):
    hi = x.astype(BF16)
    return hi, (x - hi.astype(F32)).astype(BF16)


def _mm(x, w):
    if w.dtype == BF16:
        return jnp.dot(x.astype(BF16), w, preferred_element_type=F32)
    wh, wl = _split(w)
    if x.dtype == BF16:
        return jnp.dot(x, wl, preferred_element_type=F32) + jnp.dot(x, wh, preferred_element_type=F32)
    xh, xl = _split(x)
    m = x.shape[0]
    top = jnp.dot(jnp.concatenate([xh, xl], axis=0), wh, preferred_element_type=F32)
    return (top[m:] + jnp.dot(xh, wl, preferred_element_type=F32)) + top[:m]


def _resident(shape):
    return pl.BlockSpec(shape, lambda *_: (0,) * len(shape), pipeline_mode=pl.Buffered(1))


def _log_sigmoid(x):
    return jnp.minimum(x, 0.0) - jnp.log1p(jnp.exp(-jnp.abs(x)))


def _ffn_kernel(*refs, n_ff, next_norm, emit_bf16):
    if next_norm:
        x_ref, g_ref, wg_ref, wu_ref, wd_ref, gn_ref, y_ref, hn_ref = refs[:8]
    else:
        x_ref, g_ref, wg_ref, wu_ref, wd_ref, y_ref = refs[:6]
    h_scr = refs[-1]
    j = pl.program_id(1)
    if emit_bf16:
        for dst, src in zip(refs[-4:-1], (wg_ref, wu_ref, wd_ref)):
            dst[...] = src[...].astype(BF16)

    @pl.when(j == 0)
    def _():
        h_scr[...] = _rms(x_ref[...], g_ref[...]).astype(h_scr.dtype)
        y_ref[...] = jnp.zeros(y_ref.shape, F32)

    h = h_scr[...]
    gate = _mm(h, wg_ref[...])
    up = _mm(h, wu_ref[...])
    y_ref[...] += _mm(gate * jax.nn.sigmoid(gate) * up, wd_ref[...])

    @pl.when(j == n_ff - 1)
    def _():
        y = x_ref[...] + 0.5 * y_ref[...]
        y_ref[...] = y
        if next_norm:
            hn_ref[...] = _rms(y, gn_ref[...]).astype(hn_ref.dtype)


def _ffn_half(x, g, wg, wu, wd, g_next=None, emit_bf16=False):
    m, d = x.shape
    d_ff = wg.shape[1]
    tm = _row_tile(m, ROW_TILE)
    tf = _row_tile(d_ff, FF_TILE)
    n_ff = d_ff // tf
    next_norm = g_next is not None
    row = pl.BlockSpec((tm, d), lambda i, j: (i, 0))
    vec = pl.BlockSpec((1, d), lambda i, j: (0, 0))
    w_specs = [pl.BlockSpec((d, tf), lambda i, j: (0, j)), pl.BlockSpec((d, tf), lambda i, j: (0, j)),
               pl.BlockSpec((tf, d), lambda i, j: (j, 0))]
    in_specs = [row, vec] + w_specs
    args = [x, g.reshape(1, d), wg, wu, wd]
    out_shape = [jax.ShapeDtypeStruct((m, d), F32)]
    out_specs = [row]
    if next_norm:
        in_specs.append(vec)
        args.append(g_next.reshape(1, d))
        out_shape.append(jax.ShapeDtypeStruct((m, d), wg.dtype))
        out_specs.append(row)
    if emit_bf16:
        assert m == tm and wg.dtype == F32
        out_shape += [jax.ShapeDtypeStruct(w.shape, BF16) for w in (wg, wu, wd)]
        out_specs += w_specs
    out = pl.pallas_call(
        functools.partial(_ffn_kernel, n_ff=n_ff, next_norm=next_norm, emit_bf16=emit_bf16),
        grid=(m // tm, n_ff),
        in_specs=in_specs,
        out_specs=out_specs,
        out_shape=out_shape,
        scratch_shapes=[pltpu.VMEM((tm, d), wg.dtype)],
        compiler_params=_params(("parallel", "arbitrary"), VMEM_LARGE),
        name="ffn_half",
    )(*args)
    return out if len(out) > 1 else out[0]


def _diffproj_kernel(*refs, q_scale, prompt):
    if prompt:
        h_ref, wq_ref, wk_ref, wv_ref, gq_ref, gk_ref, cos_ref, sin_ref, seg_ref, wvt_ref, qb_ref, kf_ref, vf_ref, kb_ref, vt_ref = refs
    else:
        h_ref, wq_ref, wk_ref, wv_ref, gq_ref, gk_ref, cos_ref, sin_ref, seg_ref, qb_ref, kf_ref, vf_ref = refs
    h = h_ref[...]
    tm = h.shape[0]
    heads = qb_ref.shape[1] // LANES
    cos = cos_ref[...]
    sin = sin_ref[...]
    seg = seg_ref[...]
    lane = lax.broadcasted_iota(jnp.int32, (tm, LANES), 1)
    first_half = (lane & (DH_HALF - 1)) < (DH_HALF // 2)

    def norm_rope(z, g):
        ss = _mm(z * z, seg)
        y = z * lax.rsqrt(ss * (1.0 / DH_HALF) + EPS) * g
        partner = jnp.where(first_half, pltpu.roll(y, LANES - DH_HALF // 2, axis=1), pltpu.roll(y, DH_HALF // 2, axis=1))
        return y * cos + partner * sin

    zq = _mm(h, wq_ref[...])
    gq = gq_ref[...]
    for t in range(heads):
        sl = slice(t * LANES, (t + 1) * LANES)
        qb_ref[:, sl] = (norm_rope(zq[:, sl], gq) * q_scale).astype(BF16)
    zk = _mm(h, wk_ref[...])
    gk = gk_ref[...]
    for t in range(heads):
        sl = slice(t * LANES, (t + 1) * LANES)
        kt = norm_rope(zk[:, sl], gk)
        kf_ref[:, sl] = kt
        if prompt:
            kb_ref[:, sl] = kt.astype(BF16)
    vf_ref[...] = _mm(h, wv_ref[...])
    if prompt:
        vt_ref[0] = lax.dot_general(wvt_ref[...], h, _NT, preferred_element_type=F32).astype(BF16)


def _diff_project(h, wq, wk, wv, g_q, g_k, cos, sin, q_scale, wv_t=None):
    m, d = h.shape
    tm = _row_tile(m, ROW_TILE)
    prompt = wv_t is not None
    tn = D_DIFF if wq.dtype == BF16 else 2 * LANES
    seg = jnp.kron(jnp.eye(2, dtype=F32), jnp.ones((DH_HALF, DH_HALF), F32)).astype(wq.dtype)
    row_in = pl.BlockSpec((tm, d), lambda i, j: (i, 0))
    w_spec = _resident((d, tn)) if tn == D_DIFF else pl.BlockSpec((d, tn), lambda i, j: (0, j))
    vec = pl.BlockSpec((1, LANES), lambda i, j: (0, 0))
    tab = pl.BlockSpec((tm, LANES), lambda i, j: (i, 0))
    row_out = pl.BlockSpec((tm, tn), lambda i, j: (i, j))
    in_specs = [row_in, w_spec, w_spec, w_spec, vec, vec, tab, tab, pl.BlockSpec((LANES, LANES), lambda i, j: (0, 0))]
    args = [h, wq, wk, wv, jnp.tile(g_q, 2).reshape(1, LANES), jnp.tile(g_k, 2).reshape(1, LANES), cos, sin, seg]
    out_specs = [row_out] * 3
    out_shape = [jax.ShapeDtypeStruct((m, D_DIFF), dt) for dt in (BF16, F32, F32)]
    if prompt:
        assert tn == D_DIFF
        in_specs.append(_resident((D_DIFF, d)))
        args.append(wv_t)
        out_specs += [row_out, pl.BlockSpec((1, D_DIFF, tm), lambda i, j: (i, 0, 0))]
        out_shape += [jax.ShapeDtypeStruct((m, D_DIFF), BF16), jax.ShapeDtypeStruct((m // tm, D_DIFF, tm), BF16)]
    return pl.pallas_call(
        functools.partial(_diffproj_kernel, q_scale=q_scale, prompt=prompt),
        grid=(m // tm, D_DIFF // tn),
        in_specs=in_specs,
        out_specs=out_specs,
        out_shape=out_shape,
        compiler_params=_params(("parallel", "arbitrary"), VMEM_PROJ),
        name="diff_project",
    )(*args)


def _mlstmproj_kernel(h_ref, wq_ref, wk_ref, wv_ref, wo_ref, wg_ref, b_ref, q_ref, k_ref, v_ref, og_ref, gt_ref, *,
                      transpose_gates):
    h = h_ref[...]
    tm = h.shape[0]
    q_ref[...] = _mm(h, wq_ref[...]).astype(q_ref.dtype)
    k_ref[...] = (_mm(h, wk_ref[...]) * (HEAD_DIM ** -0.5)).astype(k_ref.dtype)
    v_ref[...] = _mm(h, wv_ref[...]).astype(v_ref.dtype)
    og_ref[...] = jax.nn.sigmoid(_mm(h, wo_ref[...]))
    zg = _mm(h, wg_ref[...]) + b_ref[...]
    lane = lax.broadcasted_iota(jnp.int32, (tm, GATE_LANES), 1)
    gates = jnp.where(lane < H_MLSTM, zg, _log_sigmoid(zg))
    if transpose_gates:
        for c in range(tm // MLSTM_CHUNK):
            gt = jnp.transpose(gates[c * MLSTM_CHUNK:(c + 1) * MLSTM_CHUNK, :])
            gt_ref[c] = gt[:2 * H_MLSTM, :]
    else:
        gt_ref[...] = gates


def _mlstm_project(h, w, w_gate, bias, transpose_gates):
    m, d = h.shape
    dt = w[0].dtype
    tm = _row_tile(m, ROW_TILE)
    row_in = pl.BlockSpec((tm, d), lambda i: (i, 0))
    row_out = pl.BlockSpec((tm, D_MLSTM), lambda i: (i, 0))
    if transpose_gates:
        cpt = tm // MLSTM_CHUNK
        g_shape = jax.ShapeDtypeStruct((m // MLSTM_CHUNK, 2 * H_MLSTM, MLSTM_CHUNK), F32)
        g_spec = pl.BlockSpec((cpt, 2 * H_MLSTM, MLSTM_CHUNK), lambda i: (i, 0, 0))
    else:
        g_shape = jax.ShapeDtypeStruct((m, GATE_LANES), F32)
        g_spec = pl.BlockSpec((tm, GATE_LANES), lambda i: (i, 0))
    return pl.pallas_call(
        functools.partial(_mlstmproj_kernel, transpose_gates=transpose_gates),
        grid=(m // tm,),
        in_specs=[row_in] + [_resident((d, D_MLSTM))] * 4 + [_resident((d, GATE_LANES)), _resident((1, GATE_LANES))],
        out_specs=[row_out, row_out, row_out, row_out, g_spec],
        out_shape=[jax.ShapeDtypeStruct((m, D_MLSTM), t) for t in (dt, dt, dt, F32)] + [g_shape],
        compiler_params=_params(("parallel",), VMEM_LARGE),
        name="mlstm_project",
    )(h, *w, w_gate, bias)


def _softmax_pv(s, v):
    m = jnp.max(s, axis=-1, keepdims=True)
    p = jnp.exp(s - m)
    l = jnp.sum(p, axis=-1, keepdims=True)
    return jnp.dot(p.astype(BF16), v, preferred_element_type=F32) / l


def _memq_kernel(*refs, attend):
    if attend:
        h_ref, w_ref, gq_ref, mk_ref, mv_ref, o_ref = refs
    else:
        h_ref, w_ref, gq_ref, o_ref = refs
    z = _mm(h_ref[...], w_ref[...])
    gq = gq_ref[...]
    for t in range(H_MEM):
        sl = slice(t * HEAD_DIM, (t + 1) * HEAD_DIM)
        q = _rms(z[:, sl], gq)
        if attend:
            s = lax.dot_general(q.astype(BF16), mk_ref[:, sl], _NT, preferred_element_type=F32) * (HEAD_DIM ** -0.5)
            q = _softmax_pv(s, mv_ref[:, sl])
        o_ref[:, sl] = q.astype(BF16)


def _mem_query(h, w, g_q, mem_k=None, mem_v=None):
    m, d = h.shape
    tm = _row_tile(m, ROW_TILE)
    attend = mem_k is not None
    in_specs = [pl.BlockSpec((tm, d), lambda i: (i, 0)), pl.BlockSpec((d, D_MEMH), lambda i: (0, 0)),
                pl.BlockSpec((1, HEAD_DIM), lambda i: (0, 0))]
    args = [h, w, g_q.reshape(1, HEAD_DIM)]
    if attend:
        n_mem = mem_k.shape[0]
        in_specs += [pl.BlockSpec((n_mem, D_MEMH), lambda i: (0, 0))] * 2
        args += [mem_k, mem_v]
    return pl.pallas_call(
        functools.partial(_memq_kernel, attend=attend),
        grid=(m // tm,),
        in_specs=in_specs,
        out_specs=pl.BlockSpec((tm, D_MEMH), lambda i: (i, 0)),
        out_shape=jax.ShapeDtypeStruct((m, D_MEMH), BF16),
        compiler_params=_params(("parallel",), VMEM_SMALL),
        name="mem_query",
    )(*args)


def _memkv_kernel(mem_ref, g_ref, wk_ref, wv_ref, gk_ref, kf_ref, vf_ref, kb_ref, vb_ref):
    mn = _rms(mem_ref[...], g_ref[...]).astype(BF16)
    zk = jnp.dot(mn, wk_ref[...], preferred_element_type=F32)
    gk = gk_ref[...]
    for t in range(H_MEM):
        sl = slice(t * HEAD_DIM, (t + 1) * HEAD_DIM)
        kt = _rms(zk[:, sl], gk)
        kf_ref[:, sl] = kt
        kb_ref[:, sl] = kt.astype(BF16)
    zv = jnp.dot(mn, wv_ref[...], preferred_element_type=F32)
    vf_ref[...] = zv
    vb_ref[...] = zv.astype(BF16)


def _memory_kv(mem, g_in, wk, wv, g_k):
    n_mem, d = mem.shape
    shapes = [jax.ShapeDtypeStruct((n_mem, D_MEMH), dt) for dt in (F32, F32, BF16, BF16)]
    return pl.pallas_call(
        _memkv_kernel,
        out_shape=shapes,
        compiler_params=pltpu.CompilerParams(vmem_limit_bytes=VMEM_SMALL * MIB),
        name="memory_kv",
    )(mem, g_in.reshape(1, d), wk, wv, g_k.reshape(1, HEAD_DIM))


def _lambda_kernel(q1_ref, k1_ref, q2_ref, k2_ref, o_ref, *, lambda_init):
    a = jnp.exp(jnp.sum(q1_ref[...] * k1_ref[...], axis=-1, keepdims=True))
    b = jnp.exp(jnp.sum(q2_ref[...] * k2_ref[...], axis=-1, keepdims=True))
    o_ref[...] = a - b + lambda_init


def _diff_lambda(q1, k1, q2, k2, lambda_init):
    r = lambda a: a.reshape(1, DH_HALF).astype(F32)
    return pl.pallas_call(
        functools.partial(_lambda_kernel, lambda_init=lambda_init),
        out_shape=jax.ShapeDtypeStruct((1, 1), F32),
        name="diff_lambda",
    )(r(q1), r(k1), r(q2), r(k2))


def _split_maps(q):
    lane = lax.broadcasted_iota(jnp.int32, q.shape, 1)
    qf = q.astype(F32)
    return jnp.concatenate([jnp.where(lane < DH_HALF, qf, 0.0), jnp.where(lane >= DH_HALF, qf, 0.0)], axis=0).astype(BF16)


def _subln(d, g, out_scale):
    return _rms(d, g) * out_scale


def _attn_kernel(lam_ref, q_ref, k_ref, vt_ref, gs_ref, o_ref, m_scr, acc_scr, *, tq, out_scale):
    i = pl.program_id(1)
    qs = _split_maps(q_ref[...])
    m_scr[...] = jnp.full(m_scr.shape, NEG, F32)
    acc_scr[...] = jnp.zeros(acc_scr.shape, F32)
    tv = vt_ref.shape[2]
    ones_rows = (lax.broadcasted_iota(jnp.int32, (SUM_ROWS, tv), 0) == 0).astype(F32).astype(BF16)

    def v_tile(idx):
        return jnp.concatenate([vt_ref[idx], ones_rows], axis=0)

    def step(j, masked):
        kj = k_ref[pl.ds(pl.multiple_of(j * tq, tq), tq), :]
        s = lax.dot_general(kj, qs, _NT, preferred_element_type=F32)
        if masked:
            key = lax.broadcasted_iota(jnp.int32, s.shape, 0)
            qry = lax.broadcasted_iota(jnp.int32, s.shape, 1)
            s = jnp.where(key <= jnp.where(qry >= tq, qry - tq, qry), s, NEG)
        m_old = m_scr[...]
        m_new = jnp.maximum(m_old, jnp.max(s, axis=0, keepdims=True))
        alpha = jnp.exp2(m_old - m_new)
        pb = jnp.exp2(s - m_new).astype(BF16)
        pv = jnp.dot(v_tile(j * (tq // tv)), pb[:tv], preferred_element_type=F32)
        for u in range(1, tq // tv):
            pv += jnp.dot(v_tile(j * (tq // tv) + u), pb[u * tv:(u + 1) * tv], preferred_element_type=F32)
        acc_scr[...] = alpha * acc_scr[...] + pv
        m_scr[...] = m_new

    def body(j, carry):
        step(j, False)
        return carry

    lax.fori_loop(0, i, body, 0)
    step(i, True)

    o_t = acc_scr[:HEAD_DIM, :] / acc_scr[HEAD_DIM:HEAD_DIM + 1, :]
    d = jnp.transpose(o_t[:, :tq] - lam_ref[0, 0] * o_t[:, tq:])
    o_ref[...] = _subln(d, gs_ref[...], out_scale).astype(BF16)


def _diff_attention_prompt(lam, q, k, v_t, g_subln, out_scale):
    s_len = q.shape[0]
    tv = v_t.shape[2]
    tq = _row_tile(s_len, ATTN_TILE)
    assert tq % tv == 0
    blk = pl.BlockSpec((tq, HEAD_DIM), lambda h, i: (i, h))
    return pl.pallas_call(
        functools.partial(_attn_kernel, tq=tq, out_scale=out_scale),
        grid=(H_DIFF, s_len // tq),
        in_specs=[pl.BlockSpec(memory_space=pltpu.SMEM), blk, pl.BlockSpec((s_len, HEAD_DIM), lambda h, i: (0, h)),
                  pl.BlockSpec((s_len // tv, HEAD_DIM, tv), lambda h, i: (0, h, 0)),
                  pl.BlockSpec((1, HEAD_DIM), lambda h, i: (0, 0))],
        out_specs=blk,
        out_shape=jax.ShapeDtypeStruct((s_len, D_DIFF), BF16),
        scratch_shapes=[pltpu.VMEM((1, 2 * tq), F32), pltpu.VMEM((HEAD_DIM + SUM_ROWS, 2 * tq), F32)],
        compiler_params=_params(("parallel", "arbitrary"), VMEM_LARGE),
        name="diff_attention_prompt",
    )(lam, q, k, v_t, g_subln.reshape(1, HEAD_DIM))


def _decode_attn_kernel(pt_ref, lam_ref, q_ref, kn_ref, vn_ref, gs_ref, *refs, pages, out_scale):
    k_refs = refs[:pages]
    v_refs = refs[pages:2 * pages]
    o_ref, qs_scr, m_scr, l_scr, acc_scr = refs[2 * pages:]
    p_idx = pl.program_id(1)
    n_maps = 2 * H_DIFF
    page = k_refs[0].shape[1]

    @pl.when(p_idx == 0)
    def _():
        row = lax.broadcasted_iota(jnp.int32, (n_maps, D_DIFF), 0)
        col = lax.broadcasted_iota(jnp.int32, (n_maps, D_DIFF), 1)
        qb = jnp.broadcast_to(q_ref[0].astype(F32), (n_maps, D_DIFF))
        qs_scr[...] = jnp.where(col // DH_HALF == row, qb, 0.0).astype(BF16)
        m_scr[...] = jnp.full(m_scr.shape, NEG, F32)
        l_scr[...] = jnp.zeros(l_scr.shape, F32)
        acc_scr[...] = jnp.zeros(acc_scr.shape, F32)

    qs = qs_scr[...]
    s = jnp.concatenate(
        [lax.dot_general(qs, k_refs[t][0], _NT, preferred_element_type=F32) for t in range(pages)], axis=1)
    m_old = m_scr[...]
    m_new = jnp.maximum(m_old, jnp.max(s, axis=-1, keepdims=True))
    alpha = jnp.exp(m_old - m_new)
    p = jnp.exp(s - m_new).astype(BF16)
    l_scr[...] = alpha * l_scr[...] + jnp.sum(p.astype(F32), axis=-1, keepdims=True)
    for h in range(H_DIFF):
        sl = slice(h * HEAD_DIM, (h + 1) * HEAD_DIM)
        rows = pl.ds(h, page, stride=H_DIFF)
        pv = jnp.dot(p[:, :page], v_refs[0][0, 0, rows, :].astype(BF16), preferred_element_type=F32)
        for t in range(1, pages):
            pv += jnp.dot(p[:, t * page:(t + 1) * page], v_refs[t][0, 0, rows, :].astype(BF16),
                          preferred_element_type=F32)
        acc_scr[:, sl] = alpha * acc_scr[:, sl] + pv
    m_scr[...] = m_new

    @pl.when(p_idx == pl.num_programs(1) - 1)
    def _():
        s_new = jnp.sum(qs.astype(F32) * kn_ref[0], axis=-1, keepdims=True)
        m_fin = jnp.maximum(m_new, s_new)
        a_fin = jnp.exp(m_new - m_fin)
        p_new = jnp.exp(s_new - m_fin)
        l_fin = a_fin * l_scr[...] + p_new
        o = (a_fin * acc_scr[...] + p_new * vn_ref[0]) / l_fin
        lam = lam_ref[0, 0]
        gs = gs_ref[...]
        for h in range(H_DIFF):
            sl = slice(h * HEAD_DIM, (h + 1) * HEAD_DIM)
            d = o[2 * h:2 * h + 1, sl] - lam * o[2 * h + 1:2 * h + 2, sl]
            o_ref[0, :, sl] = _subln(d, gs, out_scale).astype(BF16)


def _diff_attention_decode(lam, q, k_new, v_new, cache_k, cache_v, layer, page_table, g_subln, out_scale,
                           pages_per_step=DECODE_PAGES):
    nb = q.shape[0]
    n_pages = page_table.shape[1]
    n_maps = 2 * H_DIFF
    page = cache_k.shape[1]
    pages = pages_per_step if n_pages % pages_per_step == 0 else 1
    tok = pl.BlockSpec((1, 1, D_DIFF), lambda b, p, pt: (b, 0, 0))

    def k_spec(t):
        return pl.BlockSpec((1, page, D_DIFF), lambda b, p, pt: (pt[b, p * pages + t], 0, 0))

    def v_spec(t):
        return pl.BlockSpec((1, 1, page * H_DIFF, HEAD_DIM), lambda b, p, pt: (layer, pt[b, p * pages + t], 0, 0))

    grid_spec = pltpu.PrefetchScalarGridSpec(
        num_scalar_prefetch=1,
        grid=(nb, n_pages // pages),
        in_specs=[pl.BlockSpec(memory_space=pltpu.SMEM), tok, tok, tok, pl.BlockSpec((1, HEAD_DIM), lambda b, p, pt: (0, 0))]
        + [k_spec(t) for t in range(pages)] + [v_spec(t) for t in range(pages)],
        out_specs=tok,
        scratch_shapes=[pltpu.VMEM((n_maps, D_DIFF), BF16), pltpu.VMEM((n_maps, 1), F32), pltpu.VMEM((n_maps, 1), F32),
                        pltpu.VMEM((n_maps, D_DIFF), F32)],
    )
    out = pl.pallas_call(
        functools.partial(_decode_attn_kernel, pages=pages, out_scale=out_scale),
        grid_spec=grid_spec,
        out_shape=jax.ShapeDtypeStruct((nb, 1, D_DIFF), BF16),
        compiler_params=_params(("parallel", "arbitrary"), VMEM_LARGE),
        name="diff_attention_decode",
    )(page_table, lam, q.reshape(nb, 1, D_DIFF), k_new.reshape(nb, 1, D_DIFF), v_new.reshape(nb, 1, D_DIFF),
      g_subln.reshape(1, HEAD_DIM), *([cache_k] * pages), *([cache_v] * pages))
    return out.reshape(nb, D_DIFF)


def _mlstm_kernel(q_ref, k_ref, v_ref, og_ref, gt_ref, go_ref, hm_ref, c_ref, n_ref, m_ref):
    @pl.when(pl.program_id(0) == 0)
    def _():
        c_ref[...] = jnp.zeros(c_ref.shape, F32)
        n_ref[...] = jnp.zeros(n_ref.shape, F32)
        m_ref[...] = jnp.zeros(m_ref.shape, F32)

    L = MLSTM_CHUNK
    lane8 = lax.broadcasted_iota(jnp.int32, (2 * H_MLSTM, L), 1)
    row = lax.broadcasted_iota(jnp.int32, (L, L), 0)
    col = lax.broadcasted_iota(jnp.int32, (L, L), 1)
    causal = col <= row

    def chunk(c, carry):
        r0 = pl.multiple_of(c * L, L)
        g8 = gt_ref[c]
        cs = g8
        d = 1
        while d < L:
            cs = cs + jnp.where(lane8 >= d, pltpu.roll(cs, d, axis=1), 0.0)
            d *= 2
        stacked = jnp.concatenate([g8, cs, jnp.zeros((L - 4 * H_MLSTM, L), F32)], axis=0)
        cols = jnp.transpose(stacked)
        for h in range(H_MLSTM):
            sl = slice(h * HEAD_DIM, (h + 1) * HEAD_DIM)
            q = q_ref[pl.ds(r0, L), sl]
            k = k_ref[pl.ds(r0, L), sl]
            v = v_ref[pl.ds(r0, L), sl]
            ig_r = g8[h:h + 1, :]
            bt_r = cs[H_MLSTM + h:H_MLSTM + h + 1, :]
            ig_c = cols[:, h:h + 1]
            bt_c = cols[:, 3 * H_MLSTM + h:3 * H_MLSTM + h + 1]
            m0 = m_ref[h:h + 1, 0:1]
            c0 = c_ref[h]
            n0 = n_ref[h:h + 1, :]

            dm = jnp.where(causal, bt_c + (ig_r - bt_r), NEG)
            inter = bt_c + m0
            m_c = jnp.maximum(inter, jnp.max(dm, axis=-1, keepdims=True))
            w_intra = jnp.exp(dm - m_c)
            w_inter = jnp.exp(inter - m_c)
            a = w_intra * lax.dot_general(q, k, _NT, preferred_element_type=F32)
            cq = lax.dot_general(q, c0.astype(BF16), _NT, preferred_element_type=F32)
            num = jnp.dot(a.astype(BF16), v, preferred_element_type=F32) + w_inter * cq
            nq = jnp.sum(q.astype(F32) * n0, axis=-1, keepdims=True)
            den = jnp.sum(a, axis=-1, keepdims=True) + w_inter * nq
            hh = num / jnp.maximum(jnp.abs(den), jnp.exp(-m_c))
            hm_ref[pl.ds(r0, L), sl] = (_rms(hh, go_ref[:, sl]) * og_ref[pl.ds(r0, L), sl]).astype(BF16)

            bl = bt_r[:, L - 1:L]
            ml = m_c[L - 1:L, :]
            w_end = jnp.exp(bl - bt_c + ig_c - ml)
            decay = jnp.exp(bl + m0 - ml)
            vw_t = jnp.transpose(w_end * v.astype(F32)).astype(BF16)
            c_ref[h] = decay * c0 + jnp.dot(vw_t, k, preferred_element_type=F32)
            n_ref[h:h + 1, :] = decay * n0 + jnp.sum(w_end * k.astype(F32), axis=0, keepdims=True)
            m_ref[h:h + 1, :] = jnp.broadcast_to(ml, (1, LANES))
        return carry

    lax.fori_loop(0, gt_ref.shape[0], chunk, 0)


def _mlstm_prompt(q, k, v, og, gates_t, g_out):
    s_len = q.shape[0]
    tm = _row_tile(s_len, ROW_TILE)
    cpt = tm // MLSTM_CHUNK
    row = pl.BlockSpec((tm, D_MLSTM), lambda t: (t, 0))
    return pl.pallas_call(
        _mlstm_kernel,
        grid=(s_len // tm,),
        in_specs=[row, row, row, row, pl.BlockSpec((cpt, 2 * H_MLSTM, MLSTM_CHUNK), lambda t: (t, 0, 0)),
                  pl.BlockSpec((1, D_MLSTM), lambda t: (0, 0))],
        out_specs=[row, pl.BlockSpec((H_MLSTM, HEAD_DIM, HEAD_DIM), lambda t: (0, 0, 0)),
                   pl.BlockSpec((H_MLSTM, HEAD_DIM), lambda t: (0, 0)), pl.BlockSpec((H_MLSTM, LANES), lambda t: (0, 0))],
        out_shape=[jax.ShapeDtypeStruct((s_len, D_MLSTM), BF16), jax.ShapeDtypeStruct((H_MLSTM, HEAD_DIM, HEAD_DIM), F32),
                   jax.ShapeDtypeStruct((H_MLSTM, HEAD_DIM), F32), jax.ShapeDtypeStruct((H_MLSTM, LANES), F32)],
        compiler_params=_params(("arbitrary",), VMEM_SMALL),
        name="mlstm_prompt",
    )(q, k, v, og, gates_t, g_out.reshape(1, D_MLSTM))


def _mlstm_step_kernel(q_ref, k_ref, v_ref, og_ref, ig_ref, lf_ref, m0_ref, n0_ref, c0_ref, go_ref,
                       hm_ref, c1_ref, n1_ref, m1_ref):
    q = q_ref[0].astype(F32)
    k = k_ref[0].astype(F32)
    v = v_ref[0].astype(F32)
    ig = ig_ref[0]
    lf = lf_ref[0]
    m0 = m0_ref[0]
    n0 = n0_ref[0]
    m1 = jnp.maximum(lf + m0, ig)
    w_i = jnp.exp(ig - m1)
    w_f = jnp.exp(lf + m0 - m1)
    a = w_i * jnp.sum(q * k, axis=-1, keepdims=True)
    cq_rows = []
    for h in range(H_MLSTM):
        qh = jnp.broadcast_to(q[h:h + 1, :], (8, HEAD_DIM)).astype(BF16)
        cq = lax.dot_general(qh, c0_ref[0, h].astype(BF16), _NT, preferred_element_type=F32)
        cq_rows.append(cq[0:1, :])
    cq = jnp.concatenate(cq_rows, axis=0)
    num = a * v + w_f * cq
    den = a + w_f * jnp.sum(n0 * q, axis=-1, keepdims=True)
    hh = num / jnp.maximum(jnp.abs(den), jnp.exp(-m1))
    hm_ref[0] = (_rms(hh, go_ref[...]) * og_ref[0]).astype(BF16)
    n1_ref[0] = w_f * n0 + w_i * k
    m1_ref[0] = m1
    wv = w_i * v
    wv_cols = jnp.transpose(jnp.concatenate([wv, jnp.zeros((HEAD_DIM - H_MLSTM, HEAD_DIM), F32)], axis=0))
    for h in range(H_MLSTM):
        c1_ref[0, h] = w_f[h:h + 1, 0:1] * c0_ref[0, h] + wv_cols[:, h:h + 1] * k[h:h + 1, :]


def _mlstm_step(q, k, v, og, ig, lf, m0, n0, c0, g_out):
    nb = q.shape[0]
    hd = (nb, H_MLSTM, HEAD_DIM)
    tok = pl.BlockSpec((1, H_MLSTM, HEAD_DIM), lambda b: (b, 0, 0))
    mat = pl.BlockSpec((1, H_MLSTM, HEAD_DIM, HEAD_DIM), lambda b: (b, 0, 0, 0))
    return pl.pallas_call(
        _mlstm_step_kernel,
        grid=(nb,),
        in_specs=[tok] * 8 + [mat, pl.BlockSpec((H_MLSTM, HEAD_DIM), lambda b: (0, 0))],
        out_specs=[tok, mat, tok, tok],
        out_shape=[jax.ShapeDtypeStruct(hd, BF16), jax.ShapeDtypeStruct((nb, H_MLSTM, HEAD_DIM, HEAD_DIM), F32),
                   jax.ShapeDtypeStruct(hd, F32), jax.ShapeDtypeStruct(hd, F32)],
        compiler_params=_params(("parallel",), VMEM_SMALL),
        name="mlstm_step",
    )(q.reshape(hd), k.reshape(hd), v.reshape(hd), og.reshape(hd), ig, lf, m0, n0, c0, g_out)


def _mem_decode_kernel(q_ref, k_ref, v_ref, o_ref):
    n_mem = k_ref.shape[2] // H_MEM
    for t in range(H_MEM):
        sl = slice(t * HEAD_DIM, (t + 1) * HEAD_DIM)
        rows = pl.ds(t, n_mem, stride=H_MEM)
        q = jnp.broadcast_to(q_ref[0, :, sl], (8, HEAD_DIM))
        s = lax.dot_general(q, k_ref[0, 0, rows, :].astype(BF16), _NT, preferred_element_type=F32) * (HEAD_DIM ** -0.5)
        o = _softmax_pv(s, v_ref[0, 0, rows, :].astype(BF16))
        o_ref[0, :, sl] = o[0:1, :].astype(BF16)


def _memory_attend_decode(q, mem_k, mem_v, layer):
    _, nb, rows, _ = mem_k.shape
    tok = pl.BlockSpec((1, 1, D_MEMH), lambda b: (b, 0, 0))
    mem = pl.BlockSpec((1, 1, rows, HEAD_DIM), lambda b: (layer, b, 0, 0))
    out = pl.pallas_call(
        _mem_decode_kernel,
        grid=(nb,),
        in_specs=[tok, mem, mem],
        out_specs=tok,
        out_shape=jax.ShapeDtypeStruct((nb, 1, D_MEMH), BF16),
        compiler_params=_params(("parallel",), VMEM_SMALL),
        name="memory_attend_decode",
    )(q.reshape(nb, 1, D_MEMH), mem_k, mem_v)
    return out.reshape(nb, D_MEMH)


def _outproj_kernel(x_ref, d_ref, m_ref, c_ref, wd_ref, wm_ref, wc_ref, y_ref):
    y = x_ref[...] + _mm(d_ref[...], wd_ref[...])
    y += _mm(m_ref[...], wm_ref[...])
    y += _mm(c_ref[...], wc_ref[...])
    y_ref[...] = y


def _out_project(x, d, m, c, w_d, w_m, w_c):
    rows, dm = x.shape
    tm = _row_tile(rows, ROW_TILE)

    def row(width):
        return pl.BlockSpec((tm, width), lambda i: (i, 0))

    return pl.pallas_call(
        _outproj_kernel,
        grid=(rows // tm,),
        in_specs=[row(dm), row(D_DIFF), row(D_MLSTM), row(D_MEMH), _resident(w_d.shape), _resident(w_m.shape),
                  _resident(w_c.shape)],
        out_specs=row(dm),
        out_shape=jax.ShapeDtypeStruct((rows, dm), F32),
        compiler_params=_params(("parallel",), VMEM_LARGE),
        name="out_project",
    )(x, d, m, c, w_d, w_m, w_c)


def _rope_tables(pos):
    inv = 1.0 / (ROPE_THETA ** (jnp.arange(0, DH_HALF, 2, dtype=F32) / DH_HALF))
    half = DH_HALF // 2
    ang = pos.astype(F32)[:, None] * jnp.tile(inv, LANES // half)[None, :]
    sign = jnp.where((jnp.arange(LANES) // half) % 2 == 0, -1.0, 1.0).astype(F32)
    return jnp.cos(ang), jnp.sin(ang) * sign[None, :]


def _split_w_in(w_in, dtype):
    o = 0
    wq, wk, wv = (w_in[:, o + t * D_DIFF:o + (t + 1) * D_DIFF].astype(dtype) for t in range(3))
    o = 3 * D_DIFF
    w_ml = tuple(w_in[:, o + t * D_MLSTM:o + (t + 1) * D_MLSTM].astype(dtype) for t in range(4))
    o += 4 * D_MLSTM
    w_gate = jnp.pad(w_in[:, o:o + 2 * H_MLSTM], ((0, 0), (0, GATE_LANES - 2 * H_MLSTM))).astype(dtype)
    o += 2 * H_MLSTM
    w_mq = w_in[:, o:o + D_MEMH].astype(dtype)
    return wq, wk, wv, w_ml, w_gate, w_mq


def _split_w_out(w_out, dtype):
    w = w_out.astype(dtype)
    return w[:D_DIFF], w[D_DIFF:D_DIFF + D_MLSTM], w[D_DIFF + D_MLSTM:]


def kernel(x_prompt, x_sample, cache_diff_k, cache_diff_v, cache_mem_k, cache_mem_v, state_mlstm_C, state_mlstm_n,
           state_mlstm_m, page_table, mem_prompt, g_ffn1, w_ffn1_gate, w_ffn1_up, w_ffn1_down, g_mix, w_in, b_igate,
           b_fgate, g_q_diff, g_k_diff, lambda_q1, lambda_k1, lambda_q2, lambda_k2, g_subln, g_mlstm_out, g_mem_in,
           w_mem_k, w_mem_v, g_q_mem, g_k_mem, w_out, g_ffn2, w_ffn2_gate, w_ffn2_up, w_ffn2_down):
    depth = w_in.shape[0]
    bp, s_len, d_model = x_prompt.shape
    nb, t_s, _ = x_sample.shape
    assert bp == 1 and t_s == 1, "one prompt sequence and one new token per decode request"
    n_phys, page = cache_diff_k.shape[1], cache_diff_k.shape[2]
    n_past = page_table.shape[1] * page
    n_mem = mem_prompt.shape[1]

    xp = x_prompt.reshape(s_len, d_model)
    xs = x_sample.reshape(nb, d_model)
    cos_p, sin_p = _rope_tables(jnp.arange(s_len))
    cos_s, sin_s = (jnp.broadcast_to(t, (nb, LANES)) for t in _rope_tables(jnp.full((1,), n_past)))

    outs = [[] for _ in range(12)]
    for l in range(depth):
        lambda_init = 0.8 - 0.6 * math.exp(-0.3 * l)
        out_scale = 1.0 - lambda_init
        bf = lambda a: a[l].astype(BF16)
        gate_bias = jnp.pad(jnp.concatenate([b_igate[l], b_fgate[l]]).astype(F32),
                            (0, GATE_LANES - 2 * H_MLSTM)).reshape(1, GATE_LANES)
        lam = _diff_lambda(lambda_q1[l], lambda_k1[l], lambda_q2[l], lambda_k2[l], lambda_init)
        g_mo = g_mlstm_out[l].reshape(H_MLSTM, HEAD_DIM)

        ffn1 = (g_ffn1[l], w_ffn1_gate[l], w_ffn1_up[l], w_ffn1_down[l])
        ffn2 = (g_ffn2[l], w_ffn2_gate[l], w_ffn2_up[l], w_ffn2_down[l])
        wq, wk, wv, w_ml, w_gate, w_mq = _split_w_in(w_in[l], F32)
        xs, hs, *ffn1_bf16 = _ffn_half(xs, *ffn1, g_next=g_mix[l], emit_bf16=True)
        dq, dk, dv = _diff_project(hs, wq, wk, wv, g_q_diff[l], g_k_diff[l], cos_s, sin_s, DH_HALF ** -0.5)
        lq, lk, lv, og, gates = _mlstm_project(hs, w_ml, w_gate, gate_bias, transpose_gates=False)
        mq = _mem_query(hs, w_mq, g_q_mem[l])
        diff_o = _diff_attention_decode(lam, dq, dk, dv, cache_diff_k[l].astype(BF16).reshape(n_phys, page, D_DIFF),
                                        cache_diff_v.reshape(depth, n_phys, page * H_DIFF, HEAD_DIM), l, page_table,
                                        g_subln[l], out_scale)
        rep = lambda a: jnp.broadcast_to(a.astype(F32)[:, :, None], (nb, H_MLSTM, HEAD_DIM))
        h_m, c_s, n_s, m_s = _mlstm_step(lq, lk, lv, og, rep(gates[:, :H_MLSTM]), rep(gates[:, H_MLSTM:2 * H_MLSTM]),
                                         rep(state_mlstm_m[l]), state_mlstm_n[l].astype(F32),
                                         state_mlstm_C[l].astype(F32), g_mo)
        mem_o = _memory_attend_decode(mq, cache_mem_k.reshape(depth, nb, n_mem * H_MEM, HEAD_DIM),
                                      cache_mem_v.reshape(depth, nb, n_mem * H_MEM, HEAD_DIM), l)
        xs = _out_project(xs, diff_o, h_m.reshape(nb, D_MLSTM), mem_o, *_split_w_out(w_out[l], F32))
        xs, *ffn2_bf16 = _ffn_half(xs, *ffn2, emit_bf16=True)
        outs[7].append(dk.reshape(nb, 1, H_DIFF, 2, DH_HALF))
        outs[8].append(dv.reshape(nb, 1, H_DIFF, HEAD_DIM))
        outs[9].append(c_s)
        outs[10].append(n_s)
        outs[11].append(m_s[:, :, 0])

        wq, wk, wv, w_ml, w_gate, w_mq = _split_w_in(w_in[l], BF16)
        xp, hp = _ffn_half(xp, g_ffn1[l], *ffn1_bf16, g_next=g_mix[l])
        dq, dk, dv, dk_b, dv_t = _diff_project(hp, wq, wk, wv, g_q_diff[l], g_k_diff[l], cos_p, sin_p,
                                               DH_HALF ** -0.5 * math.log2(math.e), wv_t=wv.T)
        lq, lk, lv, og, gates_t = _mlstm_project(hp, w_ml, w_gate, gate_bias, transpose_gates=True)
        mk, mv, mk_b, mv_b = _memory_kv(mem_prompt.reshape(n_mem, d_model), g_mem_in[l], bf(w_mem_k), bf(w_mem_v),
                                        g_k_mem[l])
        mem_o = _mem_query(hp, w_mq, g_q_mem[l], mk_b, mv_b)
        diff_o = _diff_attention_prompt(lam, dq, dk_b, dv_t, g_subln[l], out_scale)
        h_m, c_p, n_p, m_p = _mlstm_prompt(lq, lk, lv, og, gates_t, g_mo)
        xp = _out_project(xp, diff_o, h_m, mem_o, *_split_w_out(w_out[l], BF16))
        xp = _ffn_half(xp, g_ffn2[l], *ffn2_bf16)
        outs[0].append(dk.reshape(1, s_len, H_DIFF, 2, DH_HALF))
        outs[1].append(dv.reshape(1, s_len, H_DIFF, HEAD_DIM))
        outs[2].append(mk.reshape(1, n_mem, H_MEM, HEAD_DIM))
        outs[3].append(mv.reshape(1, n_mem, H_MEM, HEAD_DIM))
        outs[4].append(c_p.reshape(1, H_MLSTM, HEAD_DIM, HEAD_DIM))
        outs[5].append(n_p.reshape(1, H_MLSTM, HEAD_DIM))
        outs[6].append(m_p[:, 0].reshape(1, H_MLSTM))

    stacked = [jnp.stack(o, 0) for o in outs]
    return (xp.reshape(1, s_len, d_model), xs.reshape(nb, 1, d_model), *stacked)
```
